```python
import jax, jax.numpy as jnp
from jax import lax
import numpy as np

D_MODEL = 2048
BATCH = 8
SEQ = 4096
DEPTH = 2
DEC_BATCH = 2
DEC_SEQ = 16384
PAST_LEN = 128

N_MIXERS = 2
N_RWKV_LAYERS = (DEPTH + 1) // 2
N_ATTN_LAYERS = DEPTH // 2

RWKV_HEAD_DIM = 64
RWKV_HEADS = D_MODEL // RWKV_HEAD_DIM
LORA_DECAY = 96
LORA_ICLR = 96
LORA_GATE = 256
GN_EPS = 64e-5

ATT_HEAD_DIM = 64
ATT_HEADS = D_MODEL // ATT_HEAD_DIM
ATT_KV_HEADS = 8
ATT_GROUP = ATT_HEADS // ATT_KV_HEADS
WINDOW = 128
ATT_BLOCK = 128
ATT_SPAN = ATT_BLOCK + 2 * WINDOW
QKV_DIM = (ATT_HEADS + 2 * ATT_KV_HEADS) * ATT_HEAD_DIM
NEG_INF = -1e30

N_EXPERTS = 64
TOP_K = 6
D_EXPERT = 1408
ROUTED_SCALE = 2.446
MOE_BLOCK = 128

RMS_EPS = 1e-6

kernel_name = "hybrid_rwkv7_swa_moe_adaln_encoder"


def rmsnorm(x, g):
    x32 = x.astype(jnp.float32)
    y = x32 * lax.rsqrt(jnp.mean(x32 * x32, axis=-1, keepdims=True) + RMS_EPS)
    return (y * g.astype(jnp.float32)).astype(x.dtype)


def modulate(x, g, shift, scale):
    return rmsnorm(x, g) * (1 + scale[:, None, :]) + shift[:, None, :]


def centred_shift(h):
    zero = jnp.zeros_like(h[:, :1])
    prev = jnp.concatenate([zero, h[:, :-1]], axis=1)
    nxt = jnp.concatenate([h[:, 1:], zero], axis=1)
    return 0.5 * (prev + nxt) - h


def wkv7_scan(r, decay, k, v, kk, a, reverse):
    B, S, H, N = r.shape
    def step(state, inp):
        r_t, w_t, k_t, v_t, kk_t, a_t = inp
        sa = jnp.einsum('bhvk,bhk->bhv', state, -kk_t)
        state = (state * w_t[:, :, None, :]
                 + sa[..., None] * (kk_t * a_t)[:, :, None, :]
                 + v_t[..., None] * k_t[:, :, None, :])
        return state, jnp.einsum('bhvk,bhk->bhv', state, r_t)
    s0 = jnp.zeros((B, H, N, N), jnp.float32)
    xs = tuple(jnp.swapaxes(t, 0, 1) for t in (r, decay, k, v, kk, a))
    _, y = lax.scan(step, s0, xs, reverse=reverse)
    return jnp.swapaxes(y, 0, 1)


def rwkv7_mix(h, mu, w_r, w_k, w_v, w_o, dec_w0, dec_w1, dec_w2, iclr_a0, iclr_a1, iclr_a2,
              gate_g1, gate_g2, k_k, k_a, r_k, gn_w, gn_b):
    B, S, D = h.shape
    H, N = RWKV_HEADS, RWKV_HEAD_DIM
    f32 = jnp.float32
    xx = centred_shift(h)
    xr, xw, xk, xv, xa, xg = (h + xx * mu[j] for j in range(6))
    heads = lambda t: t.astype(f32).reshape(B, S, H, N)
    r = heads(xr @ w_r)
    k = heads(xk @ w_k)
    v = heads(xv @ w_v)
    g = jax.nn.sigmoid(xg @ gate_g1) @ gate_g2
    kk = k * k_k.astype(f32).reshape(H, N)
    kk = kk / jnp.maximum(jnp.linalg.norm(kk, axis=-1, keepdims=True), 1e-12)
    k_a_h = k_a.astype(f32).reshape(H, N)
    r_k_h = r_k.astype(f32)
    wkv, bonus = [], []
    for d in range(2):
        z = (dec_w0[d] + jnp.tanh(xw @ dec_w1[d]) @ dec_w2[d]).astype(f32)
        w = -jax.nn.softplus(-z) - 0.5
        decay = heads(jnp.exp(-jnp.exp(w)))
        a = heads(jax.nn.sigmoid((iclr_a0[d] + (xa @ iclr_a1[d]) @ iclr_a2[d]).astype(f32)))
        k_d = k * (1.0 + (a - 1.0) * k_a_h)
        wkv.append(wkv7_scan(r, decay, k_d, v, kk, a, reverse=(d == 1)))
        bonus.append(jnp.sum(r * k_d * r_k_h, axis=-1, keepdims=True) * v)
    y = wkv[0] + wkv[1]
    mean = jnp.mean(y, axis=-1, keepdims=True)
    var = jnp.var(y, axis=-1, keepdims=True)
    y = (y - mean) * lax.rsqrt(var + GN_EPS)
    y = y.reshape(B, S, D) * gn_w.astype(f32) + gn_b.astype(f32)
    y = y + (bonus[0] + bonus[1]).reshape(B, S, D)
    return (y.astype(h.dtype) * g) @ w_o


def alibi_slopes(n_heads):
    return jnp.exp2(-8.0 * jnp.arange(1, n_heads + 1, dtype=jnp.float32) / n_heads)


def window_attention(h, w_qkv, w_o, sink):
    B, S, _ = h.shape
    KV, G, DH = ATT_KV_HEADS, ATT_GROUP, ATT_HEAD_DIM
    f32 = jnp.float32
    nb = S // ATT_BLOCK
    qkv = h @ w_qkv
    q = qkv[..., :ATT_HEADS * DH].reshape(B, S, KV, G, DH) * (DH ** -0.5)
    k = qkv[..., ATT_HEADS * DH:(ATT_HEADS + KV) * DH].reshape(B, S, KV, DH)
    v = qkv[..., (ATT_HEADS + KV) * DH:].reshape(B, S, KV, DH)
    pad = ((0, 0), (WINDOW, WINDOW), (0, 0), (0, 0))
    k_pad = jnp.pad(k, pad)
    v_pad = jnp.pad(v, pad)
    q_blocks = jnp.moveaxis(q.reshape(B, nb, ATT_BLOCK, KV, G, DH), 1, 0)
    slopes = alibi_slopes(ATT_HEADS).reshape(KV, G)[:, :, None, None]
    sink_col = jnp.broadcast_to(sink.astype(f32).reshape(KV, G)[None, :, :, None, None],
                                (B, KV, G, ATT_BLOCK, 1))
    offs_q = jnp.arange(ATT_BLOCK)
    offs_k = jnp.arange(ATT_SPAN) - WINDOW

    def block(args):
        i, qb = args
        start = i * ATT_BLOCK
        kb = lax.dynamic_slice_in_dim(k_pad, start, ATT_SPAN, axis=1)
        vb = lax.dynamic_slice_in_dim(v_pad, start, ATT_SPAN, axis=1)
        s = jnp.einsum('bqkgd,blkd->bkgql', qb, kb, preferred_element_type=f32)
        q_pos = start + offs_q
        k_pos = start + offs_k
        dist = jnp.abs(q_pos[:, None] - k_pos[None, :]).astype(f32)
        valid = (dist <= WINDOW) & (k_pos[None, :] >= 0) & (k_pos[None, :] < S)
        s = jnp.where(valid, s - slopes * dist, NEG_INF)
        p = jax.nn.softmax(jnp.concatenate([s, sink_col], axis=-1), axis=-1)[..., :ATT_SPAN]
        return jnp.einsum('bkgql,blkd->bqkgd', p.astype(vb.dtype), vb)

    o = lax.map(block, (jnp.arange(nb), q_blocks))
    o = jnp.moveaxis(o, 0, 1).reshape(B, S, ATT_HEADS * DH)
    return o @ w_o


def moe_ffn(h, router_w, router_bias, w_gate, w_up, w_down, sh_gate, sh_up, sh_down):
    B, S, D = h.shape
    T = B * S
    n_pairs = T * TOP_K
    n_blocks = -(-n_pairs // MOE_BLOCK) + N_EXPERTS
    xt = h.reshape(T, D)
    scores = jax.nn.sigmoid((xt @ router_w).astype(jnp.float32))
    _, idx = lax.top_k(scores + router_bias.astype(jnp.float32), TOP_K)
    gates = jnp.take_along_axis(scores, idx, axis=-1)
    gates = gates / jnp.sum(gates, axis=-1, keepdims=True) * ROUTED_SCALE
    e_flat = idx.reshape(-1)
    order = jnp.argsort(e_flat)
    e_sorted = e_flat[order]
    tok_sorted = (order // TOP_K).astype(jnp.int32)
    gate_sorted = gates.reshape(-1)[order]
    counts = jnp.bincount(e_flat, length=N_EXPERTS)
    padded = (counts + MOE_BLOCK - 1) // MOE_BLOCK * MOE_BLOCK
    start_raw = jnp.cumsum(counts) - counts
    end_pad = jnp.cumsum(padded)
    start_pad = end_pad - padded
    dest = start_pad[e_sorted] + jnp.arange(n_pairs) - start_raw[e_sorted]
    buf_tok = jnp.full((n_blocks * MOE_BLOCK,), T, jnp.int32).at[dest].set(tok_sorted)
    buf_gate = jnp.zeros((n_blocks * MOE_BLOCK,), jnp.float32).at[dest].set(gate_sorted)
    blk_expert = jnp.minimum(
        jnp.searchsorted(end_pad, jnp.arange(n_blocks) * MOE_BLOCK, side='right'), N_EXPERTS - 1)
    x_ext = jnp.concatenate([xt, jnp.zeros((1, D), xt.dtype)], axis=0)

    def expert_block(acc, blk):
        e, tok, gate = blk
        xb = x_ext[tok]
        hb = jax.nn.silu(xb @ w_gate[e]) * (xb @ w_up[e])
        yb = (hb @ w_down[e]) * gate[:, None].astype(hb.dtype)
        return acc.at[tok].add(yb.astype(acc.dtype)), None

    acc, _ = lax.scan(expert_block, jnp.zeros((T + 1, D), xt.dtype),
                      (blk_expert, buf_tok.reshape(n_blocks, MOE_BLOCK),
                       buf_gate.reshape(n_blocks, MOE_BLOCK)))
    shared = (jax.nn.silu(xt @ sh_gate) * (xt @ sh_up)) @ sh_down
    return (acc[:T] + shared).reshape(B, S, D)


def setup_inputs(seed: int = 0) -> dict:
    key = jax.random.key(seed)
    ks = iter(jax.random.split(key, 64))
    f32 = jnp.float32
    D, E, F = D_MODEL, N_EXPERTS, D_EXPERT
    NR, NA = N_RWKV_LAYERS, N_ATTN_LAYERS

    def nrm(shape, scale=1.0):
        return scale * jax.random.normal(next(ks), shape, f32)

    def gain(shape):
        return 1.0 + 0.05 * jax.random.normal(next(ks), shape, f32)

    return {
        "x_prompt": nrm((BATCH, SEQ, D)),
        "x_sample": nrm((DEC_BATCH, DEC_SEQ, D)),
        "c_prompt": nrm((BATCH, D)),
        "c_sample": nrm((DEC_BATCH, D)),
        "ada_w": nrm((DEPTH, D, 6 * D), 0.5 * D ** -0.5),
        "ada_b": nrm((DEPTH, 6 * D), 0.02),
        "norm_mix": gain((DEPTH, D)),
        "norm_ffn": gain((DEPTH, D)),
        "rw_mu": jax.random.uniform(next(ks), (NR, 6, D), f32),
        "rw_wr": nrm((NR, D, D), D ** -0.5),
        "rw_wk": nrm((NR, D, D), D ** -0.5),
        "rw_wv": nrm((NR, D, D), D ** -0.5),
        "rw_wo": nrm((NR, D, D), D ** -0.5),
        "rw_dec_w0": jax.random.uniform(next(ks), (NR, 2, D), f32, -6.0, -1.0),
        "rw_dec_w1": nrm((NR, 2, D, LORA_DECAY), D ** -0.5),
        "rw_dec_w2": nrm((NR, 2, LORA_DECAY, D), LORA_DECAY ** -0.5),
        "rw_iclr_a0": nrm((NR, 2, D), 0.1),
        "rw_iclr_a1": nrm((NR, 2, D, LORA_ICLR), D ** -0.5),
        "rw_iclr_a2": nrm((NR, 2, LORA_ICLR, D), 0.5 * LORA_ICLR ** -0.5),
        "rw_gate_g1": nrm((NR, D, LORA_GATE), D ** -0.5),
        "rw_gate_g2": nrm((NR, LORA_GATE, D), LORA_GATE ** -0.5),
        "rw_k_k": 0.85 + 0.05 * jax.random.normal(next(ks), (NR, D), f32),
        "rw_k_a": gain((NR, D)),
        "rw_r_k": nrm((NR, RWKV_HEADS, RWKV_HEAD_DIM), 0.1),
        "rw_gn_w": gain((NR, D)),
        "rw_gn_b": nrm((NR, D), 0.02),
        "at_w_qkv": nrm((NA, D, QKV_DIM), D ** -0.5),
        "at_w_o": nrm((NA, ATT_HEADS * ATT_HEAD_DIM, D), (ATT_HEADS * ATT_HEAD_DIM) ** -0.5),
        "at_sink": nrm((NA, ATT_HEADS), 1.0),
        "router_w": nrm((DEPTH, D, E), D ** -0.5),
        "router_bias": nrm((DEPTH, E), 0.01),
        "moe_w_gate": nrm((DEPTH, E, D, F), D ** -0.5),
        "moe_w_up": nrm((DEPTH, E, D, F), D ** -0.5),
        "moe_w_down": nrm((DEPTH, E, F, D), F ** -0.5),
        "sh_w_gate": nrm((DEPTH, D, F), D ** -0.5),
        "sh_w_up": nrm((DEPTH, D, F), D ** -0.5),
        "sh_w_down": nrm((DEPTH, F, D), F ** -0.5),
        "final_norm": gain((D,)),
    }


def reference(x_prompt, x_sample, c_prompt, c_sample, ada_w, ada_b, norm_mix, norm_ffn,
              rw_mu, rw_wr, rw_wk, rw_wv, rw_wo, rw_dec_w0, rw_dec_w1, rw_dec_w2,
              rw_iclr_a0, rw_iclr_a1, rw_iclr_a2, rw_gate_g1, rw_gate_g2, rw_k_k, rw_k_a,
              rw_r_k, rw_gn_w, rw_gn_b, at_w_qkv, at_w_o, at_sink, router_w, router_bias,
              moe_w_gate, moe_w_up, moe_w_down, sh_w_gate, sh_w_up, sh_w_down, final_norm):
    ys = [x_prompt, x_sample]
    cs = [c_prompt, c_sample]
    for i in range(DEPTH):
        j = i // N_MIXERS
        if i % N_MIXERS == 0:
            rw = (rw_mu[j], rw_wr[j], rw_wk[j], rw_wv[j], rw_wo[j], rw_dec_w0[j], rw_dec_w1[j],
                  rw_dec_w2[j], rw_iclr_a0[j], rw_iclr_a1[j], rw_iclr_a2[j], rw_gate_g1[j],
                  rw_gate_g2[j], rw_k_k[j], rw_k_a[j], rw_r_k[j], rw_gn_w[j], rw_gn_b[j])
            mixer = lambda h, rw=rw: rwkv7_mix(h, *rw)
        else:
            at = (at_w_qkv[j], at_w_o[j], at_sink[j])
            mixer = lambda h, at=at: window_attention(h, *at)
        ffn = (router_w[i], router_bias[i], moe_w_gate[i], moe_w_up[i], moe_w_down[i],
               sh_w_gate[i], sh_w_up[i], sh_w_down[i])
        for grp in range(2):
            mod = jax.nn.silu(cs[grp]) @ ada_w[i] + ada_b[i]
            shift_m, scale_m, gate_m, shift_f, scale_f, gate_f = jnp.split(mod, 6, axis=-1)
            x = ys[grp]
            x = x + gate_m[:, None, :] * mixer(modulate(x, norm_mix[i], shift_m, scale_m))
            x = x + gate_f[:, None, :] * moe_ffn(modulate(x, norm_ffn[i], shift_f, scale_f), *ffn)
            ys[grp] = x
    y_prompt = rmsnorm(ys[0], final_norm)
    y_sample = rmsnorm(ys[1], final_norm)
    return (y_prompt, y_sample)
```

```python
import functools
from typing import NamedTuple

import jax
import jax.numpy as jnp
from jax import lax
from jax.experimental import pallas as pl
from jax.experimental.pallas import tpu as pltpu

F32 = jnp.float32
BF16 = jnp.bfloat16

HEAD = 64
GROUP_LANES = 256
CHUNK = 64
ATT_GROUP = 4
WINDOW = 128
ATT_BLOCK = 128
TOP_K = 6
ROUTED_SCALE = 2.446
RMS_EPS = 1e-6
GN_EPS = 64e-5
NEG_INF = -1e30
SLOT = 8
TILE_SUBLANES = 8
TILE_LANES = 128
VMEM_LIMIT = 56 * 1024 * 1024


class Seqs(NamedTuple):
    b1: int
    s1: int
    b2: int
    s2: int

    @property
    def t1(self):
        return self.b1 * self.s1

    @property
    def total(self):
        return self.b1 * self.s1 + self.b2 * self.s2

    @property
    def nseq(self):
        return self.b1 + self.b2

    def seq_of_tile(self, i, tm):
        n1 = self.t1 // tm
        return jnp.where(i < n1, i // (self.s1 // tm), self.b1 + (i - n1) // (self.s2 // tm))

    def starts_seq(self, tok):
        return jnp.where(tok < self.t1, lax.rem(tok, self.s1) == 0, lax.rem(tok - self.t1, self.s2) == 0)

    def ends_seq(self, tok_end):
        return jnp.where(tok_end <= self.t1, lax.rem(tok_end, self.s1) == 0,
                         lax.rem(tok_end - self.t1, self.s2) == 0)


def _params(*sem):
    return pltpu.CompilerParams(dimension_semantics=sem, vmem_limit_bytes=VMEM_LIMIT)


def _modnorm(x, g, shift, scale):
    y = x * lax.rsqrt(jnp.mean(x * x, axis=-1, keepdims=True) + RMS_EPS)
    return (y * g) * (1.0 + scale) + shift


def _split_bf16(x):
    hi = x.astype(BF16)
    lo = (x - hi.astype(F32)).astype(BF16)
    return hi, lo


def _dot(a, b):
    return jnp.dot(a, b, preferred_element_type=F32)


def _dot_nt(a, b):
    return lax.dot_general(a, b, (((1,), (1,)), ((), ())), preferred_element_type=F32)


def _dot_tn(a, b):
    return lax.dot_general(a, b, (((0,), (0,)), ((), ())), preferred_element_type=F32)


def _block_mask():
    r = lax.broadcasted_iota(jnp.int32, (GROUP_LANES, GROUP_LANES), 0) // HEAD
    c = lax.broadcasted_iota(jnp.int32, (GROUP_LANES, GROUP_LANES), 1) // HEAD
    return r == c


def _group_sum(x, ones_bd):
    hi, lo = _split_bf16(x)
    return _dot(hi, ones_bd) + _dot(lo, ones_bd)


def _modnorm_kernel(x_ref, g_ref, mod_ref, o_ref, *, row):
    h = _modnorm(x_ref[...], g_ref[...], mod_ref[row:row + 1, :], mod_ref[row + 1:row + 2, :])
    o_ref[...] = h.astype(o_ref.dtype)


def modnorm(x, g, mod3, seqs, *, row, tm=256):
    t, d = x.shape
    return pl.pallas_call(
        functools.partial(_modnorm_kernel, row=row),
        grid=(t // tm,),
        in_specs=[pl.BlockSpec((tm, d), lambda i: (i, 0)),
                  pl.BlockSpec((1, d), lambda i: (0, 0)),
                  pl.BlockSpec((None, 6, d), lambda i: (seqs.seq_of_tile(i, tm), 0, 0))],
        out_specs=pl.BlockSpec((tm, d), lambda i: (i, 0)),
        out_shape=jax.ShapeDtypeStruct((t, d), BF16),
        compiler_params=_params("parallel"),
    )(x, g, mod3)


def _shiftmix_kernel(x_ref, xp_ref, xn_ref, g_ref, mod_ref, mu_ref, *o_refs, seqs, tm):
    i = pl.program_id(0)
    g = g_ref[...]
    shift = mod_ref[0:1, :]
    scale = mod_ref[1:2, :]
    h = _modnorm(x_ref[...], g, shift, scale)
    hp = _modnorm(xp_ref[...], g, shift, scale)[TILE_SUBLANES - 1:TILE_SUBLANES, :]
    hn = _modnorm(xn_ref[...], g, shift, scale)[0:1, :]
    hp = jnp.where(seqs.starts_seq(i * tm), 0.0, hp)
    hn = jnp.where(seqs.ends_seq((i + 1) * tm), 0.0, hn)
    rows = lax.broadcasted_iota(jnp.int32, h.shape, 0)
    prev = jnp.where(rows == 0, hp, pltpu.roll(h, 1, 0))
    nxt = jnp.where(rows == tm - 1, hn, pltpu.roll(h, tm - 1, 0))
    xx = 0.5 * (prev + nxt) - h
    for j, o_ref in enumerate(o_refs):
        o_ref[...] = (h + xx * mu_ref[j:j + 1, :]).astype(o_ref.dtype)


def shiftmix(x, g, mod3, mu, seqs, *, tm=256):
    t, d = x.shape
    nb8 = t // TILE_SUBLANES
    r8 = tm // TILE_SUBLANES
    return pl.pallas_call(
        functools.partial(_shiftmix_kernel, seqs=seqs, tm=tm),
        grid=(t // tm,),
        in_specs=[pl.BlockSpec((tm, d), lambda i: (i, 0)),
                  pl.BlockSpec((TILE_SUBLANES, d), lambda i: (jnp.maximum(i * r8 - 1, 0), 0)),
                  pl.BlockSpec((TILE_SUBLANES, d), lambda i: (jnp.minimum((i + 1) * r8, nb8 - 1), 0)),
                  pl.BlockSpec((1, d), lambda i: (0, 0)),
                  pl.BlockSpec((None, 6, d), lambda i: (seqs.seq_of_tile(i, tm), 0, 0)),
                  pl.BlockSpec((6, d), lambda i: (0, 0))],
        out_specs=[pl.BlockSpec((tm, d), lambda i: (i, 0))] * 6,
        out_shape=[jax.ShapeDtypeStruct((t, d), BF16)] * 6,
        compiler_params=_params("parallel"),
    )(x, x, x, g, mod3, mu)


def _mm_kernel(a_ref, w_ref, *refs, epi, tn, n_extra, precision):
    extra = refs[:n_extra]
    o_ref = refs[n_extra]
    n = o_ref.shape[-1]
    a = a_ref[...]
    for c in range(n // tn):
        sl = slice(c * tn, (c + 1) * tn)
        acc = jnp.dot(a, w_ref[:, sl], preferred_element_type=F32, precision=precision)
        if epi is not None:
            acc = epi(acc, *[e[:, sl] for e in extra])
        o_ref[:, sl] = acc.astype(o_ref.dtype)


def mm(a, w, *, out_dtype, epi=None, extras=(), seqs=None, tm=512, tn=512, nb=None, precision=None):
    m, k = a.shape
    n = w.shape[1]
    nb = n if nb is None else nb
    tn = min(tn, nb)
    tm = min(tm, m)
    specs = [pl.BlockSpec((tm, k), lambda i, j: (i, 0)), pl.BlockSpec((k, nb), lambda i, j: (0, j))]
    args = [a, w]
    for arr, kind in extras:
        if kind == "col":
            specs.append(pl.BlockSpec((1, nb), lambda i, j: (0, j)))
        elif kind == "row":
            specs.append(pl.BlockSpec((tm, nb), lambda i, j: (i, j)))
        else:
            specs.append(pl.BlockSpec((None, arr.shape[1], nb),
                                      lambda i, j: (seqs.seq_of_tile(i, tm), 0, j)))
        args.append(arr)
    return pl.pallas_call(
        functools.partial(_mm_kernel, epi=epi, tn=tn, n_extra=len(extras), precision=precision),
        grid=(m // tm, n // nb),
        in_specs=specs,
        out_specs=pl.BlockSpec((tm, nb), lambda i, j: (i, j)),
        out_shape=jax.ShapeDtypeStruct((m, n), out_dtype),
        compiler_params=_params("parallel", "arbitrary"),
    )(*args)


def _epi_bias(acc, b):
    return acc + b


def _epi_tanh(acc):
    return jnp.tanh(acc)


def _epi_sigmoid(acc):
    return jax.nn.sigmoid(acc)


def _epi_log_decay(acc, w0):
    y = -(acc + w0)
    softplus = jnp.maximum(y, 0.0) + jnp.log1p(jnp.exp(-jnp.abs(y)))
    return -jnp.exp(-softplus - 0.5)


def _epi_bias_sigmoid(acc, b):
    return jax.nn.sigmoid(acc + b)


def _epi_residual(row):
    def epi(acc, x, mod):
        return x + mod[row:row + 1, :] * acc
    return epi


def _expand_bd(m, bm):
    mb = m.astype(BF16)
    return jnp.where(bm, jnp.concatenate([mb] * (GROUP_LANES // HEAD), axis=0), jnp.zeros((), BF16))


def _wkv_chunk(r, kd, v, kk, lw, a, ht, bm, reverse):
    c = CHUNK
    t_i = lax.broadcasted_iota(jnp.int32, (c, c), 0)
    s_i = lax.broadcasted_iota(jnp.int32, (c, c), 1)
    tri = (s_i >= t_i) if reverse else (s_i <= t_i)
    lw_hi, lw_lo = _split_bf16(lw)
    trib = tri.astype(BF16)
    cw = _dot(trib, lw_hi) + _dot(trib, lw_lo)
    cwx = cw - lw
    edge = cw[0:1, :] if reverse else cw[c - 1:c, :]
    e_neg = jnp.exp(-cw)
    al = -kk * jnp.exp(cwx)
    rb = r * jnp.exp(cw)
    be = kk * a * e_neg
    kb = kd * e_neg
    e_rest = jnp.exp(edge - cw)
    x = jnp.concatenate([al, rb], axis=0).astype(BF16)
    z_bd = jnp.concatenate([_expand_bd(be, bm), _expand_bd(kb, bm)], axis=0)
    a_all = _dot_nt(x, z_bd)
    t2 = lax.broadcasted_iota(jnp.int32, (c, GROUP_LANES), 0)
    s2 = lax.broadcasted_iota(jnp.int32, (c, GROUP_LANES), 1) % c
    strict = (s2 > t2) if reverse else (s2 < t2)
    incl = (s2 >= t2) if reverse else (s2 <= t2)
    a_ab = jnp.where(strict, a_all[:c, :GROUP_LANES], 0.0)
    a_ak = jnp.where(strict, a_all[:c, GROUP_LANES:], 0.0)
    a_rb = jnp.where(incl, a_all[c:, :GROUP_LANES], 0.0)
    a_rk = jnp.where(incl, a_all[c:, GROUP_LANES:], 0.0)
    p = jnp.where(s2 == t2, 1.0, 0.0) + a_ab
    ap = a_ab
    for _ in range(5):
        ap = _dot(ap.astype(BF16), _expand_bd(ap, bm))
        p = p + _dot(p.astype(BF16), _expand_bd(ap, bm))
    htb = ht.astype(BF16)
    xh = _dot_nt(x, htb)
    av = _dot(jnp.concatenate([a_ak, a_rk], axis=0).astype(BF16), _expand_bd(v, bm))
    rhs = xh[:c] + av[:c]
    u = _dot(p.astype(BF16), _expand_bd(rhs, bm))
    y = xh[c:] + av[c:] + _dot(a_rb.astype(BF16), _expand_bd(u, bm))
    zw = jnp.concatenate([kk * a * e_rest, kd * e_rest], axis=0).astype(BF16)
    uv = jnp.concatenate([u, v], axis=0).astype(BF16)
    upd = _dot_tn(uv, zw)
    ht_new = ht * jnp.exp(edge) + jnp.where(bm, upd, 0.0)
    return y, ht_new


def _wkv_kernel(rf, kf, vf, lwf, af, rb, kb, vb, lwb, ab, kk_ref, ka_ref, yf_ref, yb_ref, st_ref,
                *, seqs, ct):
    i = pl.program_id(1)
    nt = pl.num_programs(1)
    nc = ct // CHUNK
    bm = _block_mask()
    ones_bd = bm.astype(BF16)
    k_k = kk_ref[...]
    k_a = ka_ref[...]

    def one(refs, off, st_idx, reset, y_ref, reverse):
        r_ref, k_ref, v_ref, lw_ref, a_ref = refs
        rows = pl.ds(off, CHUNK)
        k = k_ref[rows, :]
        a = a_ref[rows, :]
        k2 = k * k_k
        norm = jnp.sqrt(_group_sum(k2 * k2, ones_bd))
        kk = k2 / jnp.maximum(norm, 1e-12)
        kd = k * (1.0 + (a - 1.0) * k_a)
        ht = jnp.where(reset, 0.0, st_ref[st_idx])
        y, ht = _wkv_chunk(r_ref[rows, :], kd, v_ref[rows, :], kk, lw_ref[rows, :], a, ht, bm, reverse)
        st_ref[st_idx] = ht
        y_ref[rows, :] = y

    def body(c, carry):
        off_f = pl.multiple_of(c * CHUNK, CHUNK)
        one((rf, kf, vf, lwf, af), off_f, 0, seqs.starts_seq(i * ct + off_f), yf_ref, False)
        off_b = pl.multiple_of((nc - 1 - c) * CHUNK, CHUNK)
        tok_b = (nt - 1 - i) * ct + off_b
        one((rb, kb, vb, lwb, ab), off_b, 1, seqs.ends_seq(tok_b + CHUNK), yb_ref, True)
        return carry

    lax.fori_loop(0, nc, body, 0)


def wkv(r, k, v, lw2, a2, k_k, k_a, seqs, *, ct=512):
    t, d = r.shape
    ng = d // GROUP_LANES
    nt = t // ct
    fwd = lambda g, i: (i, g)
    bwd = lambda g, i: (nt - 1 - i, g)
    fwd2 = lambda g, i: (i, g)
    bwd2 = lambda g, i: (nt - 1 - i, ng + g)
    blk = (ct, GROUP_LANES)
    par = pl.BlockSpec((1, GROUP_LANES), lambda g, i: (0, g))
    return pl.pallas_call(
        functools.partial(_wkv_kernel, seqs=seqs, ct=ct),
        grid=(ng, nt),
        in_specs=[pl.BlockSpec(blk, fwd)] * 3 + [pl.BlockSpec(blk, fwd2)] * 2
                 + [pl.BlockSpec(blk, bwd)] * 3 + [pl.BlockSpec(blk, bwd2)] * 2 + [par, par],
        out_specs=[pl.BlockSpec(blk, fwd), pl.BlockSpec(blk, bwd)],
        out_shape=[jax.ShapeDtypeStruct((t, d), F32)] * 2,
        scratch_shapes=[pltpu.VMEM((2, GROUP_LANES, GROUP_LANES), F32)],
        compiler_params=_params("parallel", "arbitrary"),
    )(r, k, v, lw2, a2, r, k, v, lw2, a2, k_k, k_a)


def _rwkv_post_kernel(yf, yb, r, k, v, af, ab, g, ka, rk, gw, gb, o_ref):
    ones_bd = _block_mask().astype(BF16)
    inv_n = 1.0 / HEAD
    y = yf[...] + yb[...]
    mean = _group_sum(y, ones_bd) * inv_n
    yc = y - mean
    var = _group_sum(yc * yc, ones_bd) * inv_n
    yn = yc * lax.rsqrt(var + GN_EPS) * gw[...] + gb[...]
    kv = k[...]
    k_a = ka[...]
    kd_sum = kv * (1.0 + (af[...] - 1.0) * k_a) + kv * (1.0 + (ab[...] - 1.0) * k_a)
    bonus = _group_sum(r[...] * kd_sum * rk[...], ones_bd) * v[...]
    o_ref[...] = ((yn + bonus) * g[...]).astype(o_ref.dtype)


def rwkv_post(yf, yb, r, k, v, a2, g, k_a, r_k, gn_w, gn_b, *, tm=512):
    t, d = r.shape
    ng = d // GROUP_LANES
    blk = pl.BlockSpec((tm, GROUP_LANES), lambda i, j: (i, j))
    blk_b = pl.BlockSpec((tm, GROUP_LANES), lambda i, j: (i, ng + j))
    par = pl.BlockSpec((1, GROUP_LANES), lambda i, j: (0, j))
    return pl.pallas_call(
        _rwkv_post_kernel,
        grid=(t // tm, ng),
        in_specs=[blk] * 5 + [blk, blk_b, blk] + [par] * 4,
        out_specs=blk,
        out_shape=jax.ShapeDtypeStruct((t, d), BF16),
        compiler_params=_params("parallel", "parallel"),
    )(yf, yb, r, k, v, a2, a2, g, k_a, r_k, gn_w, gn_b)


def _attn_kernel(sink_ref, q_ref, kp_ref, kc_ref, kn_ref, vp_ref, vc_ref, vn_ref, o_ref, *, seqs, n_heads):
    i = pl.program_id(0)
    blk = ATT_BLOCK
    span = blk + 2 * WINDOW
    first = seqs.starts_seq(i * blk)
    last = seqs.ends_seq((i + 1) * blk)
    qi = lax.broadcasted_iota(jnp.int32, (blk, span), 0)
    ci = lax.broadcasted_iota(jnp.int32, (blk, span), 1)
    dist_i = jnp.abs(qi - (ci - WINDOW))
    valid = (dist_i <= WINDOW) & ((ci >= WINDOW) | jnp.logical_not(first)) \
        & ((ci < WINDOW + blk) | jnp.logical_not(last))
    dist = dist_i.astype(F32)
    outs = []
    for kvh in range(n_heads // ATT_GROUP):
        ks = slice(kvh * HEAD, (kvh + 1) * HEAD)
        k_span = jnp.concatenate([kp_ref[:, ks], kc_ref[:, ks], kn_ref[:, ks]], axis=0)
        v_span = jnp.concatenate([vp_ref[:, ks], vc_ref[:, ks], vn_ref[:, ks]], axis=0)
        for g in range(ATT_GROUP):
            h = kvh * ATT_GROUP + g
            slope = 2.0 ** (-8.0 * (h + 1) / n_heads)
            q = q_ref[:, h * HEAD:(h + 1) * HEAD]
            s = _dot_nt(q, k_span) * (HEAD ** -0.5)
            s = jnp.where(valid, s - slope * dist, NEG_INF)
            sink = sink_ref[h]
            m = jnp.maximum(jnp.max(s, axis=-1, keepdims=True), sink)
            e = jnp.exp(s - m)
            denom = jnp.sum(e, axis=-1, keepdims=True) + jnp.exp(sink - m)
            p = e / denom
            outs.append(_dot(p.astype(BF16), v_span))
    o_ref[...] = jnp.concatenate(outs, axis=-1).astype(o_ref.dtype)


def window_attention(qkv, sink, seqs, *, d):
    t = qkv.shape[0]
    n_heads = d // HEAD
    kvw = (n_heads // ATT_GROUP) * HEAD
    nb = t // ATT_BLOCK
    kcol = d // kvw
    prev = lambda i: jnp.maximum(i - 1, 0)
    nxt = lambda i: jnp.minimum(i + 1, nb - 1)
    kv_specs = [pl.BlockSpec((ATT_BLOCK, kvw), lambda i, c=c, f=f: (f(i), c))
                for c in (kcol, kcol + 1) for f in (prev, lambda i: i, nxt)]
    return pl.pallas_call(
        functools.partial(_attn_kernel, seqs=seqs, n_heads=n_heads),
        grid=(nb,),
        in_specs=[pl.BlockSpec(memory_space=pltpu.SMEM),
                  pl.BlockSpec((ATT_BLOCK, d), lambda i: (i, 0))] + kv_specs,
        out_specs=pl.BlockSpec((ATT_BLOCK, d), lambda i: (i, 0)),
        out_shape=jax.ShapeDtypeStruct((t, d), BF16),
        compiler_params=_params("parallel"),
    )(sink, qkv, qkv, qkv, qkv, qkv, qkv, qkv)


def _pack_rows(ref, y):
    half = y.shape[1] // 2
    lo = pltpu.bitcast(y[:, :half].astype(BF16).astype(F32), jnp.uint32) >> 16
    hi = pltpu.bitcast(y[:, half:].astype(BF16).astype(F32), jnp.uint32) & jnp.uint32(0xFFFF0000)
    w = lo | hi
    for s in range(half // TILE_LANES):
        ref[:, s, :] = w[:, s * TILE_LANES:(s + 1) * TILE_LANES]


def _unpack_rows(w):
    lo = pltpu.bitcast(w << 16, F32)
    hi = pltpu.bitcast(w & jnp.uint32(0xFFFF0000), F32)
    return lo, hi


def _router_kernel(x_ref, g_ref, mod_ref, rw_ref, rb_ref, hp_ref, idx_ref, gate_ref, pos_ref, cnt_ref,
                   run_ref, *, tm):
    i = pl.program_id(0)

    @pl.when(i == 0)
    def _():
        run_ref[...] = jnp.zeros_like(run_ref)

    h = _modnorm(x_ref[...], g_ref[...], mod_ref[3:4, :], mod_ref[4:5, :])
    _pack_rows(hp_ref, h)
    logits = jnp.dot(h, rw_ref[...], preferred_element_type=F32, precision=lax.Precision.HIGHEST)
    scores = jax.nn.sigmoid(logits)
    n_e = scores.shape[1]
    lane = lax.broadcasted_iota(jnp.int32, scores.shape, 1).astype(F32)
    work = scores + rb_ref[...]
    sel = jnp.zeros_like(scores)
    hot, idx, gates = [], [], []
    for _ in range(TOP_K):
        m = jnp.max(work, axis=-1, keepdims=True)
        first = jnp.min(jnp.where(work == m, lane, float(n_e)), axis=-1, keepdims=True)
        one = lane == first
        work = jnp.where(one, -jnp.inf, work)
        hot.append(one)
        idx.append(first)
        gates.append(jnp.sum(jnp.where(one, scores, 0.0), axis=-1, keepdims=True))
        sel = sel + one.astype(F32)
    total = gates[0]
    for gj in gates[1:]:
        total = total + gj
    t_i = lax.broadcasted_iota(jnp.int32, (tm, tm), 0)
    s_i = lax.broadcasted_iota(jnp.int32, (tm, tm), 1)
    before = _dot((s_i < t_i).astype(BF16), sel.astype(BF16)) + run_ref[...]
    run_ref[...] = run_ref[...] + jnp.sum(sel, axis=0, keepdims=True)
    cnt_ref[...] = run_ref[...]
    slot = lax.broadcasted_iota(jnp.int32, (tm, SLOT), 1)
    idx_o = jnp.zeros((tm, SLOT), jnp.int32)
    gate_o = jnp.zeros((tm, SLOT), F32)
    pos_o = jnp.zeros((tm, SLOT), jnp.int32)
    for j in range(TOP_K):
        pos_j = jnp.sum(jnp.where(hot[j], before, 0.0), axis=-1, keepdims=True).astype(jnp.int32)
        idx_o = jnp.where(slot == j, idx[j].astype(jnp.int32), idx_o)
        gate_o = jnp.where(slot == j, gates[j] / total * ROUTED_SCALE, gate_o)
        pos_o = jnp.where(slot == j, pos_j, pos_o)
    idx_ref[...] = idx_o
    gate_ref[...] = gate_o
    pos_ref[...] = pos_o


def router(x, g, mod3, router_w, router_bias, seqs, *, tm=256):
    t, d = x.shape
    n_e = router_w.shape[1]
    ns = d // (2 * TILE_LANES)
    tok = lambda i: (i, 0)
    return pl.pallas_call(
        functools.partial(_router_kernel, tm=tm),
        grid=(t // tm,),
        in_specs=[pl.BlockSpec((tm, d), tok),
                  pl.BlockSpec((1, d), lambda i: (0, 0)),
                  pl.BlockSpec((None, 6, d), lambda i: (seqs.seq_of_tile(i, tm), 0, 0)),
                  pl.BlockSpec((d, n_e), lambda i: (0, 0)),
                  pl.BlockSpec((1, n_e), lambda i: (0, 0))],
        out_specs=[pl.BlockSpec((tm, ns, TILE_LANES), lambda i: (i, 0, 0)),
                   pl.BlockSpec((tm, SLOT), tok), pl.BlockSpec((tm, SLOT), tok), pl.BlockSpec((tm, SLOT), tok),
                   pl.BlockSpec((1, n_e), lambda i: (0, 0))],
        out_shape=[jax.ShapeDtypeStruct((t, ns, TILE_LANES), jnp.uint32),
                   jax.ShapeDtypeStruct((t, SLOT), jnp.int32),
                   jax.ShapeDtypeStruct((t, SLOT), F32),
                   jax.ShapeDtypeStruct((t, SLOT), jnp.int32),
                   jax.ShapeDtypeStruct((1, n_e), F32)],
        scratch_shapes=[pltpu.VMEM((1, n_e), F32)],
        compiler_params=_params("arbitrary"),
    )(x, g, mod3, router_w, router_bias)


def _dispatch_kernel(dest_ref, hp_ref, zeros_ref, xs_ref, sem, *, tm):
    del zeros_ref

    def row_copy(t, j):
        return pltpu.make_async_copy(hp_ref.at[t], xs_ref.at[dest_ref[t * SLOT + j]], sem)

    def issue(t, carry):
        for j in range(TOP_K):
            row_copy(t, j).start()
        return carry

    def drain(t, carry):
        for j in range(TOP_K):
            row_copy(t, j).wait()
        return carry

    lax.fori_loop(0, tm, issue, 0)
    lax.fori_loop(0, tm, drain, 0)


def dispatch(hp, dest_flat, n_rows, *, tm=256):
    t, ns, _ = hp.shape
    zeros = jnp.zeros((n_rows, ns, TILE_LANES), jnp.uint32)
    return pl.pallas_call(
        functools.partial(_dispatch_kernel, tm=tm),
        grid=(t // tm,),
        in_specs=[pl.BlockSpec((tm * SLOT,), lambda i: (i,), memory_space=pltpu.SMEM),
                  pl.BlockSpec((tm, ns, TILE_LANES), lambda i: (i, 0, 0)),
                  pl.BlockSpec(memory_space=pl.ANY)],
        out_specs=pl.BlockSpec(memory_space=pl.ANY),
        out_shape=jax.ShapeDtypeStruct((n_rows, ns, TILE_LANES), jnp.uint32),
        scratch_shapes=[pltpu.SemaphoreType.DMA(())],
        input_output_aliases={2: 0},
        compiler_params=_params("arbitrary"),
    )(dest_flat, hp, zeros)


def _expert_kernel(be_ref, nblk_ref, xs_ref, wg_ref, wu_ref, wd_ref, o_ref, xb_ref, *, packed_out):
    i = pl.program_id(0)
    half = xb_ref.shape[1] // 2

    @pl.when(i < nblk_ref[0])
    def _():
        for s in range(half // TILE_LANES):
            lo, hi = _unpack_rows(xs_ref[:, s, :])
            xb_ref[:, s * TILE_LANES:(s + 1) * TILE_LANES] = lo.astype(BF16)
            xb_ref[:, half + s * TILE_LANES:half + (s + 1) * TILE_LANES] = hi.astype(BF16)
        xb = xb_ref[...]
        hg = _dot(xb, wg_ref[...])
        hu = _dot(xb, wu_ref[...])
        hb = (hg * jax.nn.sigmoid(hg) * hu).astype(BF16)
        y = _dot(hb, wd_ref[...])
        if packed_out:
            _pack_rows(o_ref, y)
        else:
            o_ref[...] = y

    @pl.when(i >= nblk_ref[0])
    def _():
        o_ref[...] = jnp.zeros_like(o_ref)


def expert_ffn(xs, blk_expert, n_used, w_gate, w_up, w_down, *, bm, packed_out):
    n_rows, ns, _ = xs.shape
    d = ns * 2 * TILE_LANES
    f = w_gate.shape[2]
    nblk = n_rows // bm
    if packed_out:
        o_spec = pl.BlockSpec((bm, ns, TILE_LANES), lambda i, be, nb: (i, 0, 0))
        o_shape = jax.ShapeDtypeStruct((n_rows, ns, TILE_LANES), jnp.uint32)
    else:
        o_spec = pl.BlockSpec((bm, d), lambda i, be, nb: (i, 0))
        o_shape = jax.ShapeDtypeStruct((n_rows, d), F32)
    return pl.pallas_call(
        functools.partial(_expert_kernel, packed_out=packed_out),
        grid_spec=pltpu.PrefetchScalarGridSpec(
            num_scalar_prefetch=2,
            grid=(nblk,),
            in_specs=[pl.BlockSpec((bm, ns, TILE_LANES), lambda i, be, nb: (i, 0, 0)),
                      pl.BlockSpec((None, d, f), lambda i, be, nb: (be[i], 0, 0)),
                      pl.BlockSpec((None, d, f), lambda i, be, nb: (be[i], 0, 0)),
                      pl.BlockSpec((None, f, d), lambda i, be, nb: (be[i], 0, 0))],
            out_specs=o_spec,
            scratch_shapes=[pltpu.VMEM((bm, d), BF16)]),
        out_shape=o_shape,
        compiler_params=_params("arbitrary"),
    )(blk_expert, n_used, xs, w_gate, w_up, w_down)


def _combine_kernel(dest_ref, ys_ref, gate_ref, sh_ref, x_ref, mod_ref, o_ref, buf_ref, sem, *, tm):
    def row_copy(t, j):
        return pltpu.make_async_copy(ys_ref.at[dest_ref[t * SLOT + j]], buf_ref.at[j, t], sem)

    def issue(t, carry):
        for j in range(TOP_K):
            row_copy(t, j).start()
        return carry

    def drain(t, carry):
        for j in range(TOP_K):
            row_copy(t, j).wait()
        return carry

    lax.fori_loop(0, tm, issue, 0)
    lax.fori_loop(0, tm, drain, 0)
    gates = gate_ref[...]
    gj = [jnp.broadcast_to(gates[:, j:j + 1], (tm, TILE_LANES)) for j in range(TOP_K)]
    gate_f = mod_ref[5:6, :]
    half = x_ref.shape[1] // 2
    for s in range(half // TILE_LANES):
        acc_lo = jnp.zeros((tm, TILE_LANES), F32)
        acc_hi = jnp.zeros((tm, TILE_LANES), F32)
        for j in range(TOP_K):
            lo, hi = _unpack_rows(buf_ref[j, :, s, :])
            acc_lo = acc_lo + gj[j] * lo
            acc_hi = acc_hi + gj[j] * hi
        for acc, off in ((acc_lo, s * TILE_LANES), (acc_hi, half + s * TILE_LANES)):
            sl = slice(off, off + TILE_LANES)
            o_ref[:, sl] = x_ref[:, sl] + gate_f[:, sl] * (acc + sh_ref[:, sl])


def combine(ys, dest_flat, gates, shared, x, mod3, seqs, *, tm=128):
    t, d = x.shape
    ns = ys.shape[1]
    tok = lambda i: (i, 0)
    return pl.pallas_call(
        functools.partial(_combine_kernel, tm=tm),
        grid=(t // tm,),
        in_specs=[pl.BlockSpec((tm * SLOT,), lambda i: (i,), memory_space=pltpu.SMEM),
                  pl.BlockSpec(memory_space=pl.ANY),
                  pl.BlockSpec((tm, SLOT), tok),
                  pl.BlockSpec((tm, d), tok),
                  pl.BlockSpec((tm, d), tok),
                  pl.BlockSpec((None, 6, d), lambda i: (seqs.seq_of_tile(i, tm), 0, 0))],
        out_specs=pl.BlockSpec((tm, d), tok),
        out_shape=jax.ShapeDtypeStruct((t, d), F32),
        scratch_shapes=[pltpu.VMEM((TOP_K, tm, ns, TILE_LANES), jnp.uint32), pltpu.SemaphoreType.DMA(())],
        compiler_params=_params("arbitrary"),
    )(dest_flat, ys, gates, shared, x, mod3)


def moe_block(x, mod3, g, router_w, router_bias, w_gate, w_up, w_down, sh_gate, sh_up, sh_down, seqs,
              *, bm=256):
    t, d = x.shape
    n_e = router_w.shape[1]
    hp, idx, gates, pos, counts = router(x, g, mod3, router_w, router_bias, seqs)
    cnt = counts[0].astype(jnp.int32)
    padded = (cnt + bm - 1) // bm * bm
    end_pad = jnp.cumsum(padded)
    start_pad = end_pad - padded
    dest = (jnp.take(start_pad, idx, axis=0) + pos).reshape(-1)
    nblk = -(-(t * TOP_K) // bm) + n_e
    blk_expert = jnp.minimum(
        jnp.searchsorted(end_pad, jnp.arange(nblk, dtype=jnp.int32) * bm, side="right"), n_e - 1
    ).astype(jnp.int32)
    n_used = (end_pad[-1:] // bm).astype(jnp.int32)
    xs = dispatch(hp, dest, nblk * bm)
    ys = expert_ffn(xs, blk_expert, n_used, w_gate, w_up, w_down, bm=bm, packed_out=True)
    shared = expert_ffn(hp, jnp.zeros((t // bm,), jnp.int32), jnp.full((1,), t // bm, jnp.int32),
                        sh_gate[None], sh_up[None], sh_down[None], bm=bm, packed_out=False)
    return combine(ys, dest, gates, shared, x, mod3, seqs)


def _final_norm_kernel(x_ref, g_ref, o_ref):
    x = x_ref[...]
    o_ref[...] = x * lax.rsqrt(jnp.mean(x * x, axis=-1, keepdims=True) + RMS_EPS) * g_ref[...]


def final_rmsnorm(x, g, row0, rows, *, tm=256):
    d = x.shape[1]
    b0 = row0 // tm
    return pl.pallas_call(
        _final_norm_kernel,
        grid=(rows // tm,),
        in_specs=[pl.BlockSpec((tm, d), lambda i: (b0 + i, 0)), pl.BlockSpec((1, d), lambda i: (0, 0))],
        out_specs=pl.BlockSpec((tm, d), lambda i: (i, 0)),
        out_shape=jax.ShapeDtypeStruct((rows, d), F32),
        compiler_params=_params("parallel"),
    )(x, g)


def _block_diag2(w):
    k, n = w.shape[1:]
    z = jnp.zeros((k, n), w.dtype)
    return jnp.concatenate([jnp.concatenate([w[0], z], axis=1), jnp.concatenate([z, w[1]], axis=1)], axis=0)


def rwkv_block(x, mod3, g, seqs, mu, w_r, w_k, w_v, w_o, dec_w0, dec_w1, dec_w2, iclr_a0, iclr_a1, iclr_a2,
               gate_g1, gate_g2, k_k, k_a, r_k, gn_w, gn_b):
    d = x.shape[1]
    bf = lambda w: w.astype(BF16)
    xr, xw, xk, xv, xa, xg = shiftmix(x, g, mod3, mu, seqs)
    r = mm(xr, bf(w_r), out_dtype=F32)
    k = mm(xk, bf(w_k), out_dtype=F32)
    v = mm(xv, bf(w_v), out_dtype=F32)
    lw_in = mm(xw, bf(jnp.concatenate([dec_w1[0], dec_w1[1]], axis=1)), out_dtype=BF16, epi=_epi_tanh)
    la_in = mm(xa, bf(jnp.concatenate([iclr_a1[0], iclr_a1[1]], axis=1)), out_dtype=BF16)
    lg_in = mm(xg, bf(gate_g1), out_dtype=BF16, epi=_epi_sigmoid)
    gate = mm(lg_in, bf(gate_g2), out_dtype=F32)
    lw2 = mm(lw_in, bf(_block_diag2(dec_w2)), out_dtype=F32, epi=_epi_log_decay,
             extras=[(dec_w0.reshape(1, 2 * d), "col")])
    a2 = mm(la_in, bf(_block_diag2(iclr_a2)), out_dtype=F32, epi=_epi_bias_sigmoid,
            extras=[(iclr_a0.reshape(1, 2 * d), "col")])
    k_a2 = k_a.reshape(1, d)
    yf, yb = wkv(r, k, v, lw2, a2, k_k.reshape(1, d), k_a2, seqs)
    yo = rwkv_post(yf, yb, r, k, v, a2, gate, k_a2, r_k.reshape(1, d), gn_w.reshape(1, d), gn_b.reshape(1, d))
    return mm(yo, bf(w_o), out_dtype=F32, epi=_epi_residual(2), extras=[(x, "row"), (mod3, "seq")], seqs=seqs)


def attention_block(x, mod3, g, seqs, w_qkv, w_o, sink):
    d = x.shape[1]
    h = modnorm(x, g, mod3, seqs, row=0)
    qkv = mm(h, w_qkv.astype(BF16), out_dtype=BF16)
    o = window_attention(qkv, sink, seqs, d=d)
    return mm(o, w_o.astype(BF16), out_dtype=F32, epi=_epi_residual(2), extras=[(x, "row"), (mod3, "seq")],
              seqs=seqs)


def kernel(x_prompt, x_sample, c_prompt, c_sample, ada_w, ada_b, norm_mix, norm_ffn, rw_mu, rw_wr, rw_wk, rw_wv, rw_wo, rw_dec_w0, rw_dec_w1, rw_dec_w2, rw_iclr_a0, rw_iclr_a1, rw_iclr_a2, rw_gate_g1, rw_gate_g2, rw_k_k, rw_k_a, rw_r_k, rw_gn_w, rw_gn_b, at_w_qkv, at_w_o, at_sink, router_w, router_bias, moe_w_gate, moe_w_up, moe_w_down, sh_w_gate, sh_w_up, sh_w_down, final_norm):
    b1, s1, d = x_prompt.shape
    b2, s2, _ = x_sample.shape
    seqs = Seqs(b1, s1, b2, s2)
    depth = ada_w.shape[0]
    x = jnp.concatenate([x_prompt.reshape(b1 * s1, d), x_sample.reshape(b2 * s2, d)], axis=0)
    c = jnp.concatenate([c_prompt, c_sample], axis=0)
    rows = -(-seqs.nseq // TILE_SUBLANES) * TILE_SUBLANES
    c_act = jnp.pad(jax.nn.silu(c), ((0, rows - seqs.nseq), (0, 0)))
    for i in range(depth):
        j = i // 2
        mod = mm(c_act, ada_w[i], out_dtype=F32, epi=_epi_bias, extras=[(ada_b[i].reshape(1, -1), "col")],
                 nb=d, precision=lax.Precision.HIGHEST)
        mod3 = mod[:seqs.nseq].reshape(seqs.nseq, 6, d)
        g_mix = norm_mix[i].reshape(1, d)
        if i % 2 == 0:
            x = rwkv_block(x, mod3, g_mix, seqs, rw_mu[j], rw_wr[j], rw_wk[j], rw_wv[j], rw_wo[j], rw_dec_w0[j],
                           rw_dec_w1[j], rw_dec_w2[j], rw_iclr_a0[j], rw_iclr_a1[j], rw_iclr_a2[j],
                           rw_gate_g1[j], rw_gate_g2[j], rw_k_k[j], rw_k_a[j], rw_r_k[j], rw_gn_w[j], rw_gn_b[j])
        else:
            x = attention_block(x, mod3, g_mix, seqs, at_w_qkv[j], at_w_o[j], at_sink[j])
        x = moe_block(x, mod3, norm_ffn[i].reshape(1, d), router_w[i], router_bias[i].reshape(1, -1),
                      moe_w_gate[i].astype(BF16), moe_w_up[i].astype(BF16), moe_w_down[i].astype(BF16),
                      sh_w_gate[i].astype(BF16), sh_w_up[i].astype(BF16), sh_w_down[i].astype(BF16), seqs)
    fw = final_norm.reshape(1, d)
    y1 = final_rmsnorm(x, fw, 0, seqs.t1).reshape(b1, s1, d)
    y2 = final_rmsnorm(x, fw, seqs.t1, b2 * s2).reshape(b2, s2, d)
    return (y1, y2)
```

```python
import functools
from typing import NamedTuple

import jax
import jax.numpy as jnp
from jax import lax
from jax.experimental import pallas as pl
from jax.experimental.pallas import tpu as pltpu

F32 = jnp.float32
BF16 = jnp.bfloat16

HEAD = 64
GROUP_LANES = 256
CHUNK = 64
ATT_GROUP = 4
ATT_KV_BATCH = 2
WINDOW = 128
ATT_BLOCK = 128
TOP_K = 6
ROUTED_SCALE = 2.446
RMS_EPS = 1e-6
GN_EPS = 64e-5
NEG_INF = -1e30
SLOT = 8
TILE_SUBLANES = 8
TILE_LANES = 128
VMEM_LIMIT = 56 * 1024 * 1024


class Seqs(NamedTuple):
    b1: int
    s1: int
    b2: int
    s2: int

    @property
    def t1(self):
        return self.b1 * self.s1

    @property
    def total(self):
        return self.b1 * self.s1 + self.b2 * self.s2

    @property
    def nseq(self):
        return self.b1 + self.b2

    def seq_of_tile(self, i, tm):
        n1 = self.t1 // tm
        return jnp.where(i < n1, i // (self.s1 // tm), self.b1 + (i - n1) // (self.s2 // tm))

    def starts_seq(self, tok):
        return jnp.where(tok < self.t1, lax.rem(tok, self.s1) == 0, lax.rem(tok - self.t1, self.s2) == 0)

    def ends_seq(self, tok_end):
        return jnp.where(tok_end <= self.t1, lax.rem(tok_end, self.s1) == 0,
                         lax.rem(tok_end - self.t1, self.s2) == 0)


def _params(*sem):
    return pltpu.CompilerParams(dimension_semantics=sem, vmem_limit_bytes=VMEM_LIMIT)


def _modnorm(x, g, shift, scale):
    y = x * lax.rsqrt(jnp.mean(x * x, axis=-1, keepdims=True) + RMS_EPS)
    return (y * g) * (1.0 + scale) + shift


def _split_bf16(x):
    hi = x.astype(BF16)
    lo = (x - hi.astype(F32)).astype(BF16)
    return hi, lo


def _dot(a, b):
    return jnp.dot(a, b, preferred_element_type=F32)


def _dot_nt(a, b):
    return lax.dot_general(a, b, (((1,), (1,)), ((), ())), preferred_element_type=F32)


def _dot_tn(a, b):
    return lax.dot_general(a, b, (((0,), (0,)), ((), ())), preferred_element_type=F32)


def _block_mask():
    r = lax.broadcasted_iota(jnp.int32, (GROUP_LANES, GROUP_LANES), 0) // HEAD
    c = lax.broadcasted_iota(jnp.int32, (GROUP_LANES, GROUP_LANES), 1) // HEAD
    return r == c


def _group_sum(x, ones_bd):
    hi, lo = _split_bf16(x)
    return _dot(hi, ones_bd) + _dot(lo, ones_bd)


def _modnorm_kernel(x_ref, g_ref, mod_ref, o_ref, *, row):
    h = _modnorm(x_ref[...], g_ref[...], mod_ref[row:row + 1, :], mod_ref[row + 1:row + 2, :])
    o_ref[...] = h.astype(o_ref.dtype)


def modnorm(x, g, mod3, seqs, *, row, tm=256):
    t, d = x.shape
    return pl.pallas_call(
        functools.partial(_modnorm_kernel, row=row),
        grid=(t // tm,),
        in_specs=[pl.BlockSpec((tm, d), lambda i: (i, 0)),
                  pl.BlockSpec((1, d), lambda i: (0, 0)),
                  pl.BlockSpec((None, 6, d), lambda i: (seqs.seq_of_tile(i, tm), 0, 0))],
        out_specs=pl.BlockSpec((tm, d), lambda i: (i, 0)),
        out_shape=jax.ShapeDtypeStruct((t, d), BF16),
        name="modnorm",
        compiler_params=_params("parallel"),
    )(x, g, mod3)


def _shiftmix_kernel(x_ref, xp_ref, xn_ref, g_ref, mod_ref, mu_ref, *o_refs, seqs, tm):
    i = pl.program_id(0)
    g = g_ref[...]
    shift = mod_ref[0:1, :]
    scale = mod_ref[1:2, :]
    h = _modnorm(x_ref[...], g, shift, scale)
    hp = _modnorm(xp_ref[...], g, shift, scale)[TILE_SUBLANES - 1:TILE_SUBLANES, :]
    hn = _modnorm(xn_ref[...], g, shift, scale)[0:1, :]
    hp = jnp.where(seqs.starts_seq(i * tm), 0.0, hp)
    hn = jnp.where(seqs.ends_seq((i + 1) * tm), 0.0, hn)
    rows = lax.broadcasted_iota(jnp.int32, h.shape, 0)
    prev = jnp.where(rows == 0, hp, pltpu.roll(h, 1, 0))
    nxt = jnp.where(rows == tm - 1, hn, pltpu.roll(h, tm - 1, 0))
    xx = 0.5 * (prev + nxt) - h
    for j, o_ref in enumerate(o_refs):
        o_ref[...] = (h + xx * mu_ref[j:j + 1, :]).astype(o_ref.dtype)


def shiftmix(x, g, mod3, mu, seqs, *, tm=256):
    t, d = x.shape
    nb8 = t // TILE_SUBLANES
    r8 = tm // TILE_SUBLANES
    return pl.pallas_call(
        functools.partial(_shiftmix_kernel, seqs=seqs, tm=tm),
        grid=(t // tm,),
        in_specs=[pl.BlockSpec((tm, d), lambda i: (i, 0)),
                  pl.BlockSpec((TILE_SUBLANES, d), lambda i: (jnp.maximum(i * r8 - 1, 0), 0)),
                  pl.BlockSpec((TILE_SUBLANES, d), lambda i: (jnp.minimum((i + 1) * r8, nb8 - 1), 0)),
                  pl.BlockSpec((1, d), lambda i: (0, 0)),
                  pl.BlockSpec((None, 6, d), lambda i: (seqs.seq_of_tile(i, tm), 0, 0)),
                  pl.BlockSpec((6, d), lambda i: (0, 0))],
        out_specs=[pl.BlockSpec((tm, d), lambda i: (i, 0))] * 6,
        out_shape=[jax.ShapeDtypeStruct((t, d), BF16)] * 6,
        name="rwkv_shiftmix",
        compiler_params=_params("parallel"),
    )(x, x, x, g, mod3, mu)


def _mm_kernel(a_ref, w_ref, *refs, epi, tn, n_extra, precision):
    extra = refs[:n_extra]
    o_ref = refs[n_extra]
    n = o_ref.shape[-1]
    a = a_ref[...]
    for c in range(n // tn):
        sl = slice(c * tn, (c + 1) * tn)
        acc = jnp.dot(a, w_ref[:, sl], preferred_element_type=F32, precision=precision)
        if epi is not None:
            acc = epi(acc, *[e[:, sl] for e in extra])
        o_ref[:, sl] = acc.astype(o_ref.dtype)


def mm(a, w, *, out_dtype, name, epi=None, extras=(), seqs=None, tm=512, tn=512, nb=None, precision=None):
    m, k = a.shape
    n = w.shape[1]
    nb = n if nb is None else nb
    tn = max([c for c in range(TILE_LANES, min(tn, nb) + 1, TILE_LANES) if nb % c == 0], default=nb)
    tm = min(tm, m)
    specs = [pl.BlockSpec((tm, k), lambda i, j: (i, 0)), pl.BlockSpec((k, nb), lambda i, j: (0, j))]
    args = [a, w]
    for arr, kind in extras:
        if kind == "col":
            specs.append(pl.BlockSpec((1, nb), lambda i, j: (0, j)))
        elif kind == "row":
            specs.append(pl.BlockSpec((tm, nb), lambda i, j: (i, j)))
        else:
            specs.append(pl.BlockSpec((None, arr.shape[1], nb),
                                      lambda i, j: (seqs.seq_of_tile(i, tm), 0, j)))
        args.append(arr)
    return pl.pallas_call(
        functools.partial(_mm_kernel, epi=epi, tn=tn, n_extra=len(extras), precision=precision),
        grid=(m // tm, n // nb),
        in_specs=specs,
        out_specs=pl.BlockSpec((tm, nb), lambda i, j: (i, j)),
        out_shape=jax.ShapeDtypeStruct((m, n), out_dtype),
        name=name,
        compiler_params=_params("parallel", "arbitrary"),
    )(*args)


def _epi_bias(acc, b):
    return acc + b


def _epi_tanh(acc):
    return jnp.tanh(acc)


def _epi_sigmoid(acc):
    return jax.nn.sigmoid(acc)


def _epi_log_decay(acc, w0):
    y = -(acc + w0)
    softplus = jnp.maximum(y, 0.0) + jnp.log1p(jnp.exp(-jnp.abs(y)))
    return -jnp.exp(-softplus - 0.5)


def _epi_bias_sigmoid(acc, b):
    return jax.nn.sigmoid(acc + b)


def _epi_residual(row):
    def epi(acc, x, mod):
        return x + mod[row:row + 1, :] * acc
    return epi


def _expand_bd(m, bm):
    mb = m.astype(BF16)
    return jnp.where(bm, jnp.concatenate([mb] * (GROUP_LANES // HEAD), axis=0), jnp.zeros((), BF16))


def _wkv_chunks(chains, bm, ones_bd):
    c = CHUNK
    n = len(chains)
    rev = [ch[8] for ch in chains]
    t_i = lax.broadcasted_iota(jnp.int32, (c, c), 0)
    s_i = lax.broadcasted_iota(jnp.int32, (c, c), 1)
    tri = {False: (s_i <= t_i).astype(BF16), True: (s_i >= t_i).astype(BF16)}
    t2 = lax.broadcasted_iota(jnp.int32, (c, GROUP_LANES), 0)
    s2 = lax.broadcasted_iota(jnp.int32, (c, GROUP_LANES), 1) % c
    strict = {False: s2 < t2, True: s2 > t2}
    incl = {False: s2 <= t2, True: s2 >= t2}
    eye = jnp.where(s2 == t2, 1.0, 0.0)

    k2 = [ch[1] * ch[5] for ch in chains]
    ss = [_group_sum(q * q, ones_bd) for q in k2]
    kk = [q / jnp.maximum(jnp.sqrt(s), 1e-12) for q, s in zip(k2, ss)]
    kd = [ch[1] * (1.0 + (ch[4] - 1.0) * ch[6]) for ch in chains]
    split = [_split_bf16(ch[3]) for ch in chains]
    cw = [_dot(tri[rv], hi) + _dot(tri[rv], lo) for (hi, lo), rv in zip(split, rev)]
    edge = [w[0:1, :] if rv else w[c - 1:c, :] for w, rv in zip(cw, rev)]
    e_neg = [jnp.exp(-w) for w in cw]
    al = [-q * jnp.exp(w - ch[3]) for q, w, ch in zip(kk, cw, chains)]
    rb = [ch[0] * jnp.exp(w) for w, ch in zip(cw, chains)]
    ka = [q * ch[4] for q, ch in zip(kk, chains)]
    be = [q * e for q, e in zip(ka, e_neg)]
    kb = [q * e for q, e in zip(kd, e_neg)]
    x = [jnp.concatenate([p, q], axis=0).astype(BF16) for p, q in zip(al, rb)]
    z_bd = [jnp.concatenate([_expand_bd(p, bm), _expand_bd(q, bm)], axis=0) for p, q in zip(be, kb)]
    a_all = [_dot_nt(p, q) for p, q in zip(x, z_bd)]
    a_ab = [jnp.where(strict[rv], m[:c, :GROUP_LANES], 0.0) for m, rv in zip(a_all, rev)]
    a_kr = [jnp.concatenate([jnp.where(strict[rv], m[:c, GROUP_LANES:], 0.0),
                             jnp.where(incl[rv], m[c:, GROUP_LANES:], 0.0)], axis=0).astype(BF16)
            for m, rv in zip(a_all, rev)]
    a_rb = [jnp.where(incl[rv], m[c:, :GROUP_LANES], 0.0).astype(BF16) for m, rv in zip(a_all, rev)]
    p = [eye + m for m in a_ab]
    ap = a_ab
    for _ in range(5):
        ap = [_dot(m.astype(BF16), _expand_bd(m, bm)) for m in ap]
        p = [q + _dot(q.astype(BF16), _expand_bd(m, bm)) for q, m in zip(p, ap)]
    xh = [_dot_nt(q, ch[7].astype(BF16)) for q, ch in zip(x, chains)]
    av = [_dot(m, _expand_bd(ch[2], bm)) for m, ch in zip(a_kr, chains)]
    u = [_dot(q.astype(BF16), _expand_bd(h[:c] + w[:c], bm)) for q, h, w in zip(p, xh, av)]
    y = [h[c:] + w[c:] + _dot(m, _expand_bd(q, bm)) for h, w, m, q in zip(xh, av, a_rb, u)]
    e_rest = [jnp.exp(e - w) for e, w in zip(edge, cw)]
    zw = [jnp.concatenate([p_ * e, q * e], axis=0).astype(BF16) for p_, q, e in zip(ka, kd, e_rest)]
    uv = [jnp.concatenate([q, ch[2]], axis=0).astype(BF16) for q, ch in zip(u, chains)]
    upd = [_dot_tn(p_, q) for p_, q in zip(uv, zw)]
    ht_new = [chains[j][7] * jnp.exp(edge[j]) + jnp.where(bm, upd[j], 0.0) for j in range(n)]
    return y, ht_new


def _wkv_kernel(rf, kf, vf, lwf, af, rb, kb, vb, lwb, ab, kk_ref, ka_ref, yf_ref, yb_ref, st_ref,
                *, seqs, ct, ngb):
    i = pl.program_id(1)
    nt = pl.num_programs(1)
    nc = ct // CHUNK
    bm = _block_mask()
    ones_bd = bm.astype(BF16)

    def load(refs, lanes, off, st_idx, reset, reverse):
        rows = pl.ds(off, CHUNK)
        r, k, v, lw, a = [ref[rows, lanes] for ref in refs]
        ht = jnp.where(reset, 0.0, st_ref[st_idx])
        return (r, k, v, lw, a, kk_ref[:, lanes], ka_ref[:, lanes], ht, reverse)

    def body(c, carry):
        off_f = pl.multiple_of(c * CHUNK, CHUNK)
        off_b = pl.multiple_of((nc - 1 - c) * CHUNK, CHUNK)
        tok_b = (nt - 1 - i) * ct + off_b
        reset_f = seqs.starts_seq(i * ct + off_f)
        reset_b = seqs.ends_seq(tok_b + CHUNK)
        lanes = [slice(g * GROUP_LANES, (g + 1) * GROUP_LANES) for g in range(ngb)]
        chains = [load((rf, kf, vf, lwf, af), lanes[g], off_f, g, reset_f, False) for g in range(ngb)]
        chains += [load((rb, kb, vb, lwb, ab), lanes[g], off_b, ngb + g, reset_b, True) for g in range(ngb)]
        y, ht = _wkv_chunks(chains, bm, ones_bd)
        for g in range(ngb):
            st_ref[g] = ht[g]
            st_ref[ngb + g] = ht[ngb + g]
            yf_ref[pl.ds(off_f, CHUNK), lanes[g]] = y[g]
            yb_ref[pl.ds(off_b, CHUNK), lanes[g]] = y[ngb + g]
        return carry

    lax.fori_loop(0, nc, body, 0)


def wkv(r, k, v, lw2, a2, k_k, k_a, seqs, *, ct=256, ngb=4):
    t, d = r.shape
    ngb = min(ngb, d // GROUP_LANES)
    wl = ngb * GROUP_LANES
    nlb = d // wl
    nt = t // ct
    fwd = lambda g, i: (i, g)
    bwd = lambda g, i: (nt - 1 - i, g)
    bwd2 = lambda g, i: (nt - 1 - i, nlb + g)
    blk = (ct, wl)
    par = pl.BlockSpec((1, wl), lambda g, i: (0, g))
    return pl.pallas_call(
        functools.partial(_wkv_kernel, seqs=seqs, ct=ct, ngb=ngb),
        grid=(nlb, nt),
        in_specs=[pl.BlockSpec(blk, fwd)] * 5 + [pl.BlockSpec(blk, bwd)] * 3 + [pl.BlockSpec(blk, bwd2)] * 2
                 + [par, par],
        out_specs=[pl.BlockSpec(blk, fwd), pl.BlockSpec(blk, bwd)],
        out_shape=[jax.ShapeDtypeStruct((t, d), F32)] * 2,
        scratch_shapes=[pltpu.VMEM((2 * ngb, GROUP_LANES, GROUP_LANES), F32)],
        name="wkv_scan",
        compiler_params=_params("parallel", "arbitrary"),
    )(r, k, v, lw2, a2, r, k, v, lw2, a2, k_k, k_a)


def _rwkv_post_kernel(yf, yb, r, k, v, af, ab, g, ka, rk, gw, gb, o_ref):
    ones_bd = _block_mask().astype(BF16)
    inv_n = 1.0 / HEAD
    y = yf[...] + yb[...]
    mean = _group_sum(y, ones_bd) * inv_n
    yc = y - mean
    var = _group_sum(yc * yc, ones_bd) * inv_n
    yn = yc * lax.rsqrt(var + GN_EPS) * gw[...] + gb[...]
    kv = k[...]
    k_a = ka[...]
    kd_sum = kv * (1.0 + (af[...] - 1.0) * k_a) + kv * (1.0 + (ab[...] - 1.0) * k_a)
    bonus = _group_sum(r[...] * kd_sum * rk[...], ones_bd) * v[...]
    o_ref[...] = ((yn + bonus) * g[...]).astype(o_ref.dtype)


def rwkv_post(yf, yb, r, k, v, a2, g, k_a, r_k, gn_w, gn_b, *, tm=512):
    t, d = r.shape
    ng = d // GROUP_LANES
    blk = pl.BlockSpec((tm, GROUP_LANES), lambda i, j: (i, j))
    blk_b = pl.BlockSpec((tm, GROUP_LANES), lambda i, j: (i, ng + j))
    par = pl.BlockSpec((1, GROUP_LANES), lambda i, j: (0, j))
    return pl.pallas_call(
        _rwkv_post_kernel,
        grid=(t // tm, ng),
        in_specs=[blk] * 5 + [blk, blk_b, blk] + [par] * 4,
        out_specs=blk,
        out_shape=jax.ShapeDtypeStruct((t, d), BF16),
        name="rwkv_post",
        compiler_params=_params("parallel", "parallel"),
    )(yf, yb, r, k, v, a2, a2, g, k_a, r_k, gn_w, gn_b)


def _attn_kernel(sink_ref, q_ref, kp_ref, kc_ref, kn_ref, vp_ref, vc_ref, vn_ref, o_ref, *, seqs, n_heads):
    i = pl.program_id(0)
    blk = ATT_BLOCK
    span = blk + 2 * WINDOW
    first = seqs.starts_seq(i * blk)
    last = seqs.ends_seq((i + 1) * blk)
    qi = lax.broadcasted_iota(jnp.int32, (blk, span), 0)
    ci = lax.broadcasted_iota(jnp.int32, (blk, span), 1)
    dist_i = jnp.abs(qi - (ci - WINDOW))
    valid = (dist_i <= WINDOW) & ((ci >= WINDOW) | jnp.logical_not(first)) \
        & ((ci < WINDOW + blk) | jnp.logical_not(last))
    dist = dist_i.astype(F32)
    n_kv = n_heads // ATT_GROUP
    kv_batch = min(ATT_KV_BATCH, n_kv)
    outs = []
    for kv0 in range(0, n_kv, kv_batch):
        heads = [(kvh, kvh * ATT_GROUP + g) for kvh in range(kv0, kv0 + kv_batch) for g in range(ATT_GROUP)]
        k_span, v_span = {}, {}
        for kvh in range(kv0, kv0 + kv_batch):
            ks = slice(kvh * HEAD, (kvh + 1) * HEAD)
            k_span[kvh] = jnp.concatenate([kp_ref[:, ks], kc_ref[:, ks], kn_ref[:, ks]], axis=0)
            v_span[kvh] = jnp.concatenate([vp_ref[:, ks], vc_ref[:, ks], vn_ref[:, ks]], axis=0)
        s = [_dot_nt(q_ref[:, h * HEAD:(h + 1) * HEAD], k_span[kvh]) for kvh, h in heads]
        s = [jnp.where(valid, sc * (HEAD ** -0.5) - (2.0 ** (-8.0 * (h + 1) / n_heads)) * dist, NEG_INF)
             for sc, (_, h) in zip(s, heads)]
        m = [jnp.maximum(jnp.max(sc, axis=-1, keepdims=True), sink_ref[h]) for sc, (_, h) in zip(s, heads)]
        e = [jnp.exp(sc - mx) for sc, mx in zip(s, m)]
        denom = [jnp.sum(ex, axis=-1, keepdims=True) + jnp.exp(sink_ref[h] - mx)
                 for ex, mx, (_, h) in zip(e, m, heads)]
        pv = [_dot(ex.astype(BF16), v_span[kvh]) for ex, (kvh, _) in zip(e, heads)]
        outs += [o / dn for o, dn in zip(pv, denom)]
    o_ref[...] = jnp.concatenate(outs, axis=-1).astype(o_ref.dtype)


def window_attention(qkv, sink, seqs, *, d):
    t = qkv.shape[0]
    n_heads = d // HEAD
    kvw = (n_heads // ATT_GROUP) * HEAD
    nb = t // ATT_BLOCK
    kcol = d // kvw
    prev = lambda i: jnp.maximum(i - 1, 0)
    nxt = lambda i: jnp.minimum(i + 1, nb - 1)
    kv_specs = [pl.BlockSpec((ATT_BLOCK, kvw), lambda i, c=c, f=f: (f(i), c))
                for c in (kcol, kcol + 1) for f in (prev, lambda i: i, nxt)]
    return pl.pallas_call(
        functools.partial(_attn_kernel, seqs=seqs, n_heads=n_heads),
        grid=(nb,),
        in_specs=[pl.BlockSpec(memory_space=pltpu.SMEM),
                  pl.BlockSpec((ATT_BLOCK, d), lambda i: (i, 0))] + kv_specs,
        out_specs=pl.BlockSpec((ATT_BLOCK, d), lambda i: (i, 0)),
        out_shape=jax.ShapeDtypeStruct((t, d), BF16),
        name="window_attention",
        compiler_params=_params("parallel"),
    )(sink, qkv, qkv, qkv, qkv, qkv, qkv, qkv)


def _pack_rows(ref, y):
    m = y.shape[0]
    half = y.shape[1] // 2
    ns = half // TILE_LANES
    lo = pltpu.bitcast(y[:, :half].astype(BF16).astype(F32), jnp.uint32) >> 16
    hi = pltpu.bitcast(y[:, half:].astype(BF16).astype(F32), jnp.uint32) & jnp.uint32(0xFFFF0000)
    w = lo | hi
    for s in range(ns):
        ref[pl.ds(s, m, stride=ns), :] = w[:, s * TILE_LANES:(s + 1) * TILE_LANES]


def _unpack_rows(ref, s, m, ns):
    w = ref[pl.ds(s, m, stride=ns), :]
    lo = pltpu.bitcast(w << 16, F32)
    hi = pltpu.bitcast(w & jnp.uint32(0xFFFF0000), F32)
    return lo, hi


def _router_kernel(x_ref, g_ref, mod_ref, rw_ref, rb_ref, hp_ref, idx_ref, gate_ref, pos_ref, cnt_ref,
                   run_ref, *, tm):
    i = pl.program_id(0)

    @pl.when(i == 0)
    def _():
        run_ref[...] = jnp.zeros_like(run_ref)

    h = _modnorm(x_ref[...], g_ref[...], mod_ref[3:4, :], mod_ref[4:5, :])
    _pack_rows(hp_ref, h)
    logits = jnp.dot(h, rw_ref[...], preferred_element_type=F32, precision=lax.Precision.HIGHEST)
    scores = jax.nn.sigmoid(logits)
    n_e = scores.shape[1]
    lane = lax.broadcasted_iota(jnp.int32, scores.shape, 1).astype(F32)
    work = scores + rb_ref[...]
    sel = jnp.zeros_like(scores)
    hot, idx, gates = [], [], []
    for _ in range(TOP_K):
        m = jnp.max(work, axis=-1, keepdims=True)
        first = jnp.min(jnp.where(work == m, lane, float(n_e)), axis=-1, keepdims=True)
        one = lane == first
        work = jnp.where(one, -jnp.inf, work)
        hot.append(one)
        idx.append(first)
        gates.append(jnp.sum(jnp.where(one, scores, 0.0), axis=-1, keepdims=True))
        sel = sel + one.astype(F32)
    total = gates[0]
    for gj in gates[1:]:
        total = total + gj
    t_i = lax.broadcasted_iota(jnp.int32, (tm, tm), 0)
    s_i = lax.broadcasted_iota(jnp.int32, (tm, tm), 1)
    before = _dot((s_i < t_i).astype(BF16), sel.astype(BF16)) + run_ref[...]
    run_ref[...] = run_ref[...] + jnp.sum(sel, axis=0, keepdims=True)
    cnt_ref[...] = run_ref[...]
    slot = lax.broadcasted_iota(jnp.int32, (tm, SLOT), 1)
    idx_o = jnp.zeros((tm, SLOT), jnp.int32)
    gate_o = jnp.zeros((tm, SLOT), F32)
    pos_o = jnp.zeros((tm, SLOT), jnp.int32)
    for j in range(TOP_K):
        pos_j = jnp.sum(jnp.where(hot[j], before, 0.0), axis=-1, keepdims=True).astype(jnp.int32)
        idx_o = jnp.where(slot == j, idx[j].astype(jnp.int32), idx_o)
        gate_o = jnp.where(slot == j, gates[j] / total * ROUTED_SCALE, gate_o)
        pos_o = jnp.where(slot == j, pos_j, pos_o)
    idx_ref[...] = idx_o
    gate_ref[...] = gate_o
    pos_ref[...] = pos_o


def router(x, g, mod3, router_w, router_bias, seqs, *, tm=256):
    t, d = x.shape
    n_e = router_w.shape[1]
    ns = d // (2 * TILE_LANES)
    tok = lambda i: (i, 0)
    return pl.pallas_call(
        functools.partial(_router_kernel, tm=tm),
        grid=(t // tm,),
        in_specs=[pl.BlockSpec((tm, d), tok),
                  pl.BlockSpec((1, d), lambda i: (0, 0)),
                  pl.BlockSpec((None, 6, d), lambda i: (seqs.seq_of_tile(i, tm), 0, 0)),
                  pl.BlockSpec((d, n_e), lambda i: (0, 0)),
                  pl.BlockSpec((1, n_e), lambda i: (0, 0))],
        out_specs=[pl.BlockSpec((tm * ns, TILE_LANES), tok),
                   pl.BlockSpec((tm, SLOT), tok), pl.BlockSpec((tm, SLOT), tok), pl.BlockSpec((tm, SLOT), tok),
                   pl.BlockSpec((1, n_e), lambda i: (0, 0))],
        out_shape=[jax.ShapeDtypeStruct((t * ns, TILE_LANES), jnp.uint32),
                   jax.ShapeDtypeStruct((t, SLOT), jnp.int32),
                   jax.ShapeDtypeStruct((t, SLOT), F32),
                   jax.ShapeDtypeStruct((t, SLOT), jnp.int32),
                   jax.ShapeDtypeStruct((1, n_e), F32)],
        scratch_shapes=[pltpu.VMEM((1, n_e), F32)],
        name="moe_router",
        compiler_params=_params("arbitrary"),
    )(x, g, mod3, router_w, router_bias)


def _dispatch_kernel(dest_ref, hp_ref, zeros_ref, xs_ref, sem, *, tm, ns):
    del zeros_ref

    def row_copy(t, j):
        src = hp_ref.at[pl.ds(pl.multiple_of(t * ns, ns), ns)]
        dst = xs_ref.at[pl.ds(pl.multiple_of(dest_ref[t * SLOT + j], ns), ns)]
        return pltpu.make_async_copy(src, dst, sem)

    def issue(t, carry):
        for j in range(TOP_K):
            row_copy(t, j).start()
        return carry

    def drain(t, carry):
        for j in range(TOP_K):
            row_copy(t, j).wait()
        return carry

    lax.fori_loop(0, tm, issue, 0)
    lax.fori_loop(0, tm, drain, 0)


def dispatch(hp, dest_rows, n_rows, ns, *, tm=256):
    t = hp.shape[0] // ns
    zeros = jnp.zeros((n_rows * ns, TILE_LANES), jnp.uint32)
    return pl.pallas_call(
        functools.partial(_dispatch_kernel, tm=tm, ns=ns),
        grid=(t // tm,),
        in_specs=[pl.BlockSpec((tm * SLOT,), lambda i: (i,), memory_space=pltpu.SMEM),
                  pl.BlockSpec((tm * ns, TILE_LANES), lambda i: (i, 0)),
                  pl.BlockSpec(memory_space=pl.ANY)],
        out_specs=pl.BlockSpec(memory_space=pl.ANY),
        out_shape=jax.ShapeDtypeStruct((n_rows * ns, TILE_LANES), jnp.uint32),
        scratch_shapes=[pltpu.SemaphoreType.DMA(())],
        input_output_aliases={2: 0},
        name="moe_dispatch",
        compiler_params=_params("arbitrary"),
    )(dest_rows, hp, zeros)


def _expert_kernel(be_ref, nblk_ref, xs_ref, wg_ref, wu_ref, wd_ref, o_ref, xb_ref, *, packed_out):
    i = pl.program_id(0)
    bm = xb_ref.shape[0]
    half = xb_ref.shape[1] // 2
    ns = half // TILE_LANES

    @pl.when(i < nblk_ref[0])
    def _():
        for s in range(ns):
            lo, hi = _unpack_rows(xs_ref, s, bm, ns)
            xb_ref[:, s * TILE_LANES:(s + 1) * TILE_LANES] = lo.astype(BF16)
            xb_ref[:, half + s * TILE_LANES:half + (s + 1) * TILE_LANES] = hi.astype(BF16)
        xb = xb_ref[...]
        hg = _dot(xb, wg_ref[...])
        hu = _dot(xb, wu_ref[...])
        hb = (hg * jax.nn.sigmoid(hg) * hu).astype(BF16)
        y = _dot(hb, wd_ref[...])
        if packed_out:
            _pack_rows(o_ref, y)
        else:
            o_ref[...] = y

    @pl.when(i >= nblk_ref[0])
    def _():
        o_ref[...] = jnp.zeros_like(o_ref)


def expert_ffn(xs, blk_expert, n_used, w_gate, w_up, w_down, *, bm, packed_out, name):
    d, f = w_gate.shape[1:]
    ns = d // (2 * TILE_LANES)
    n_rows = xs.shape[0] // ns
    nblk = n_rows // bm
    if packed_out:
        o_spec = pl.BlockSpec((bm * ns, TILE_LANES), lambda i, be, nb: (i, 0))
        o_shape = jax.ShapeDtypeStruct((n_rows * ns, TILE_LANES), jnp.uint32)
    else:
        o_spec = pl.BlockSpec((bm, d), lambda i, be, nb: (i, 0))
        o_shape = jax.ShapeDtypeStruct((n_rows, d), F32)
    return pl.pallas_call(
        functools.partial(_expert_kernel, packed_out=packed_out),
        grid_spec=pltpu.PrefetchScalarGridSpec(
            num_scalar_prefetch=2,
            grid=(nblk,),
            in_specs=[pl.BlockSpec((bm * ns, TILE_LANES), lambda i, be, nb: (i, 0)),
                      pl.BlockSpec((None, d, f), lambda i, be, nb: (be[i], 0, 0)),
                      pl.BlockSpec((None, d, f), lambda i, be, nb: (be[i], 0, 0)),
                      pl.BlockSpec((None, f, d), lambda i, be, nb: (be[i], 0, 0))],
            out_specs=o_spec,
            scratch_shapes=[pltpu.VMEM((bm, d), BF16)]),
        out_shape=o_shape,
        name=name,
        compiler_params=_params("arbitrary"),
    )(blk_expert, n_used, xs, w_gate, w_up, w_down)


def _combine_kernel(dest_ref, dest_next_ref, ys_ref, gate_ref, sh_ref, x_ref, mod_ref, o_ref, buf_ref, sems,
                    *, tm, ns):
    i = pl.program_id(0)
    n = pl.num_programs(0)
    slot = lax.rem(i, 2)

    def row_copy(dref, t, j, sl):
        src = ys_ref.at[pl.ds(pl.multiple_of(dref[t * SLOT + j], ns), ns)]
        dst = buf_ref.at[sl * TOP_K + j, pl.ds(pl.multiple_of(t * ns, ns), ns)]
        return pltpu.make_async_copy(src, dst, sems.at[sl])

    def gather(dref, sl):
        def issue(t, carry):
            for j in range(TOP_K):
                row_copy(dref, t, j, sl).start()
            return carry
        lax.fori_loop(0, tm, issue, 0)

    @pl.when(i == 0)
    def _():
        gather(dest_ref, 0)

    @pl.when(i + 1 < n)
    def _():
        gather(dest_next_ref, 1 - slot)

    def drain(t, carry):
        for j in range(TOP_K):
            row_copy(dest_ref, t, j, slot).wait()
        return carry

    lax.fori_loop(0, tm, drain, 0)
    gates = gate_ref[...]
    gj = [jnp.broadcast_to(gates[:, j:j + 1], (tm, TILE_LANES)) for j in range(TOP_K)]
    gate_f = mod_ref[5:6, :]
    half = x_ref.shape[1] // 2
    for s in range(ns):
        acc_lo = jnp.zeros((tm, TILE_LANES), F32)
        acc_hi = jnp.zeros((tm, TILE_LANES), F32)
        for j in range(TOP_K):
            lo, hi = _unpack_rows(buf_ref.at[slot * TOP_K + j], s, tm, ns)
            acc_lo = acc_lo + gj[j] * lo
            acc_hi = acc_hi + gj[j] * hi
        for acc, off in ((acc_lo, s * TILE_LANES), (acc_hi, half + s * TILE_LANES)):
            sl = slice(off, off + TILE_LANES)
            o_ref[:, sl] = x_ref[:, sl] + gate_f[:, sl] * (acc + sh_ref[:, sl])


def combine(ys, dest_rows, gates, shared, x, mod3, seqs, *, tm=128):
    t, d = x.shape
    ns = d // (2 * TILE_LANES)
    n = t // tm
    tok = lambda i: (i, 0)
    return pl.pallas_call(
        functools.partial(_combine_kernel, tm=tm, ns=ns),
        grid=(n,),
        in_specs=[pl.BlockSpec((tm * SLOT,), lambda i: (i,), memory_space=pltpu.SMEM),
                  pl.BlockSpec((tm * SLOT,), lambda i: (jnp.minimum(i + 1, n - 1),), memory_space=pltpu.SMEM),
                  pl.BlockSpec(memory_space=pl.ANY),
                  pl.BlockSpec((tm, SLOT), tok),
                  pl.BlockSpec((tm, d), tok),
                  pl.BlockSpec((tm, d), tok),
                  pl.BlockSpec((None, 6, d), lambda i: (seqs.seq_of_tile(i, tm), 0, 0))],
        out_specs=pl.BlockSpec((tm, d), tok),
        out_shape=jax.ShapeDtypeStruct((t, d), F32),
        scratch_shapes=[pltpu.VMEM((2 * TOP_K, tm * ns, TILE_LANES), jnp.uint32),
                        pltpu.SemaphoreType.DMA((2,))],
        name="moe_combine",
        compiler_params=_params("arbitrary"),
    )(dest_rows, dest_rows, ys, gates, shared, x, mod3)


def moe_block(x, mod3, g, router_w, router_bias, w_gate, w_up, w_down, sh_gate, sh_up, sh_down, seqs,
              *, bm=256):
    t, d = x.shape
    n_e = router_w.shape[1]
    hp, idx, gates, pos, counts = router(x, g, mod3, router_w, router_bias, seqs)
    cnt = counts[0].astype(jnp.int32)
    padded = (cnt + bm - 1) // bm * bm
    end_pad = jnp.cumsum(padded)
    start_pad = end_pad - padded
    ns = d // (2 * TILE_LANES)
    dest_rows = ((jnp.take(start_pad, idx, axis=0) + pos) * ns).reshape(-1)
    nblk = -(-(t * TOP_K) // bm) + n_e
    blk_expert = jnp.minimum(
        jnp.searchsorted(end_pad, jnp.arange(nblk, dtype=jnp.int32) * bm, side="right"), n_e - 1
    ).astype(jnp.int32)
    n_used = (end_pad[-1:] // bm).astype(jnp.int32)
    xs = dispatch(hp, dest_rows, nblk * bm, ns)
    ys = expert_ffn(xs, blk_expert, n_used, w_gate, w_up, w_down, bm=bm, packed_out=True, name="moe_experts")
    shared = expert_ffn(hp, jnp.zeros((t // bm,), jnp.int32), jnp.full((1,), t // bm, jnp.int32),
                        sh_gate[None], sh_up[None], sh_down[None], bm=bm, packed_out=False,
                        name="moe_shared_expert")
    return combine(ys, dest_rows, gates, shared, x, mod3, seqs)


def _final_norm_kernel(x_ref, g_ref, o_ref):
    x = x_ref[...]
    o_ref[...] = x * lax.rsqrt(jnp.mean(x * x, axis=-1, keepdims=True) + RMS_EPS) * g_ref[...]


def final_rmsnorm(x, g, row0, rows, *, tm=256):
    d = x.shape[1]
    b0 = row0 // tm
    return pl.pallas_call(
        _final_norm_kernel,
        grid=(rows // tm,),
        in_specs=[pl.BlockSpec((tm, d), lambda i: (b0 + i, 0)), pl.BlockSpec((1, d), lambda i: (0, 0))],
        out_specs=pl.BlockSpec((tm, d), lambda i: (i, 0)),
        out_shape=jax.ShapeDtypeStruct((rows, d), F32),
        name="final_norm",
        compiler_params=_params("parallel"),
    )(x, g)


def _block_diag2(w):
    k, n = w.shape[1:]
    z = jnp.zeros((k, n), w.dtype)
    return jnp.concatenate([jnp.concatenate([w[0], z], axis=1), jnp.concatenate([z, w[1]], axis=1)], axis=0)


def rwkv_block(x, mod3, g, seqs, mu, w_r, w_k, w_v, w_o, dec_w0, dec_w1, dec_w2, iclr_a0, iclr_a1, iclr_a2,
               gate_g1, gate_g2, k_k, k_a, r_k, gn_w, gn_b):
    d = x.shape[1]
    bf = lambda w: w.astype(BF16)
    xr, xw, xk, xv, xa, xg = shiftmix(x, g, mod3, mu, seqs)
    r = mm(xr, bf(w_r), out_dtype=F32, name="rwkv_proj")
    k = mm(xk, bf(w_k), out_dtype=F32, name="rwkv_proj")
    v = mm(xv, bf(w_v), out_dtype=F32, name="rwkv_proj")
    lw_in = mm(xw, bf(jnp.concatenate([dec_w1[0], dec_w1[1]], axis=1)), out_dtype=BF16, epi=_epi_tanh,
               name="rwkv_decay_lora_in")
    la_in = mm(xa, bf(jnp.concatenate([iclr_a1[0], iclr_a1[1]], axis=1)), out_dtype=BF16, name="rwkv_iclr_lora_in")
    lg_in = mm(xg, bf(gate_g1), out_dtype=BF16, epi=_epi_sigmoid, name="rwkv_gate_lora_in")
    gate = mm(lg_in, bf(gate_g2), out_dtype=F32, name="rwkv_gate_lora_out")
    lw2 = mm(lw_in, bf(_block_diag2(dec_w2)), out_dtype=F32, epi=_epi_log_decay,
             extras=[(dec_w0.reshape(1, 2 * d), "col")], name="rwkv_log_decay")
    a2 = mm(la_in, bf(_block_diag2(iclr_a2)), out_dtype=F32, epi=_epi_bias_sigmoid,
            extras=[(iclr_a0.reshape(1, 2 * d), "col")], name="rwkv_iclr")
    k_a2 = k_a.reshape(1, d)
    yf, yb = wkv(r, k, v, lw2, a2, k_k.reshape(1, d), k_a2, seqs)
    yo = rwkv_post(yf, yb, r, k, v, a2, gate, k_a2, r_k.reshape(1, d), gn_w.reshape(1, d), gn_b.reshape(1, d))
    return mm(yo, bf(w_o), out_dtype=F32, epi=_epi_residual(2), extras=[(x, "row"), (mod3, "seq")], seqs=seqs,
              name="mixer_out_proj")


def attention_block(x, mod3, g, seqs, w_qkv, w_o, sink):
    d = x.shape[1]
    h = modnorm(x, g, mod3, seqs, row=0)
    qkv = mm(h, w_qkv.astype(BF16), out_dtype=BF16, name="attn_qkv")
    o = window_attention(qkv, sink, seqs, d=d)
    return mm(o, w_o.astype(BF16), out_dtype=F32, epi=_epi_residual(2), extras=[(x, "row"), (mod3, "seq")],
              seqs=seqs, name="mixer_out_proj")


def kernel(x_prompt, x_sample, c_prompt, c_sample, ada_w, ada_b, norm_mix, norm_ffn, rw_mu, rw_wr, rw_wk, rw_wv, rw_wo, rw_dec_w0, rw_dec_w1, rw_dec_w2, rw_iclr_a0, rw_iclr_a1, rw_iclr_a2, rw_gate_g1, rw_gate_g2, rw_k_k, rw_k_a, rw_r_k, rw_gn_w, rw_gn_b, at_w_qkv, at_w_o, at_sink, router_w, router_bias, moe_w_gate, moe_w_up, moe_w_down, sh_w_gate, sh_w_up, sh_w_down, final_norm):
    b1, s1, d = x_prompt.shape
    b2, s2, _ = x_sample.shape
    seqs = Seqs(b1, s1, b2, s2)
    depth = ada_w.shape[0]
    x = jnp.concatenate([x_prompt.reshape(b1 * s1, d), x_sample.reshape(b2 * s2, d)], axis=0)
    c = jnp.concatenate([c_prompt, c_sample], axis=0)
    rows = -(-seqs.nseq // TILE_SUBLANES) * TILE_SUBLANES
    c_act = jnp.pad(jax.nn.silu(c), ((0, rows - seqs.nseq), (0, 0)))
    for i in range(depth):
        j = i // 2
        mod = mm(c_act, ada_w[i], out_dtype=F32, epi=_epi_bias, extras=[(ada_b[i].reshape(1, -1), "col")],
                 nb=d, precision=lax.Precision.HIGHEST, name="adaln_mod")
        mod3 = mod[:seqs.nseq].reshape(seqs.nseq, 6, d)
        g_mix = norm_mix[i].reshape(1, d)
        if i % 2 == 0:
            x = rwkv_block(x, mod3, g_mix, seqs, rw_mu[j], rw_wr[j], rw_wk[j], rw_wv[j], rw_wo[j], rw_dec_w0[j],
                           rw_dec_w1[j], rw_dec_w2[j], rw_iclr_a0[j], rw_iclr_a1[j], rw_iclr_a2[j],
                           rw_gate_g1[j], rw_gate_g2[j], rw_k_k[j], rw_k_a[j], rw_r_k[j], rw_gn_w[j], rw_gn_b[j])
        else:
            x = attention_block(x, mod3, g_mix, seqs, at_w_qkv[j], at_w_o[j], at_sink[j])
        x = moe_block(x, mod3, norm_ffn[i].reshape(1, d), router_w[i], router_bias[i].reshape(1, -1),
                      moe_w_gate[i].astype(BF16), moe_w_up[i].astype(BF16), moe_w_down[i].astype(BF16),
                      sh_w_gate[i].astype(BF16), sh_w_up[i].astype(BF16), sh_w_down[i].astype(BF16), seqs)
    fw = final_norm.reshape(1, d)
    y1 = final_rmsnorm(x, fw, 0, seqs.t1).reshape(b1, s1, d)
    y2 = final_rmsnorm(x, fw, seqs.t1, b2 * s2).reshape(b2, s2, d)
    return (y1, y2)
```

```python
import functools
from typing import NamedTuple

import jax
import jax.numpy as jnp
from jax import lax
from jax.experimental import pallas as pl
from jax.experimental.pallas import tpu as pltpu

F32 = jnp.float32
BF16 = jnp.bfloat16

HEAD = 64
GROUP_LANES = 256
CHUNK = 64
ATT_GROUP = 4
ATT_KV_BATCH = 1
WINDOW = 128
ATT_BLOCK = 128
TOP_K = 6
ROUTED_SCALE = 2.446
RMS_EPS = 1e-6
GN_EPS = 64e-5
NEG_INF = -1e30
SLOT = 8
DMA_LOOP_UNROLL = 8
TILE_SUBLANES = 8
TILE_LANES = 128
VMEM_LIMIT = 56 * 1024 * 1024


class Seqs(NamedTuple):
    b1: int
    s1: int
    b2: int
    s2: int

    @property
    def t1(self):
        return self.b1 * self.s1

    @property
    def total(self):
        return self.b1 * self.s1 + self.b2 * self.s2

    @property
    def nseq(self):
        return self.b1 + self.b2

    def seq_of_tile(self, i, tm):
        n1 = self.t1 // tm
        return jnp.where(i < n1, i // (self.s1 // tm), self.b1 + (i - n1) // (self.s2 // tm))

    def starts_seq(self, tok):
        return jnp.where(tok < self.t1, lax.rem(tok, self.s1) == 0, lax.rem(tok - self.t1, self.s2) == 0)

    def ends_seq(self, tok_end):
        return jnp.where(tok_end <= self.t1, lax.rem(tok_end, self.s1) == 0,
                         lax.rem(tok_end - self.t1, self.s2) == 0)


def _params(*sem):
    return pltpu.CompilerParams(dimension_semantics=sem, vmem_limit_bytes=VMEM_LIMIT)


def _modnorm(x, g, shift, scale):
    y = x * lax.rsqrt(jnp.mean(x * x, axis=-1, keepdims=True) + RMS_EPS)
    return (y * g) * (1.0 + scale) + shift


def _split_bf16(x):
    hi = x.astype(BF16)
    lo = (x - hi.astype(F32)).astype(BF16)
    return hi, lo


def _dot(a, b):
    return jnp.dot(a, b, preferred_element_type=F32)


def _dot_nt(a, b):
    return lax.dot_general(a, b, (((1,), (1,)), ((), ())), preferred_element_type=F32)


def _dot_tn(a, b):
    return lax.dot_general(a, b, (((0,), (0,)), ((), ())), preferred_element_type=F32)


def _block_mask():
    r = lax.broadcasted_iota(jnp.int32, (GROUP_LANES, GROUP_LANES), 0) // HEAD
    c = lax.broadcasted_iota(jnp.int32, (GROUP_LANES, GROUP_LANES), 1) // HEAD
    return r == c


def _group_sum(x, ones_bd):
    hi, lo = _split_bf16(x)
    return _dot(hi, ones_bd) + _dot(lo, ones_bd)


def _modnorm_kernel(x_ref, g_ref, mod_ref, o_ref, *, row):
    h = _modnorm(x_ref[...], g_ref[...], mod_ref[row:row + 1, :], mod_ref[row + 1:row + 2, :])
    o_ref[...] = h.astype(o_ref.dtype)


def modnorm(x, g, mod3, seqs, *, row, tm=256):
    t, d = x.shape
    return pl.pallas_call(
        functools.partial(_modnorm_kernel, row=row),
        grid=(t // tm,),
        in_specs=[pl.BlockSpec((tm, d), lambda i: (i, 0)),
                  pl.BlockSpec((1, d), lambda i: (0, 0)),
                  pl.BlockSpec((None, 6, d), lambda i: (seqs.seq_of_tile(i, tm), 0, 0))],
        out_specs=pl.BlockSpec((tm, d), lambda i: (i, 0)),
        out_shape=jax.ShapeDtypeStruct((t, d), BF16),
        name="modnorm",
        compiler_params=_params("parallel"),
    )(x, g, mod3)


def _shiftmix_kernel(x_ref, xp_ref, xn_ref, g_ref, mod_ref, mu_ref, *o_refs, seqs, tm):
    i = pl.program_id(0)
    g = g_ref[...]
    shift = mod_ref[0:1, :]
    scale = mod_ref[1:2, :]
    h = _modnorm(x_ref[...], g, shift, scale)
    hp = _modnorm(xp_ref[...], g, shift, scale)[TILE_SUBLANES - 1:TILE_SUBLANES, :]
    hn = _modnorm(xn_ref[...], g, shift, scale)[0:1, :]
    hp = jnp.where(seqs.starts_seq(i * tm), 0.0, hp)
    hn = jnp.where(seqs.ends_seq((i + 1) * tm), 0.0, hn)
    rows = lax.broadcasted_iota(jnp.int32, h.shape, 0)
    prev = jnp.where(rows == 0, hp, pltpu.roll(h, 1, 0))
    nxt = jnp.where(rows == tm - 1, hn, pltpu.roll(h, tm - 1, 0))
    xx = 0.5 * (prev + nxt) - h
    for j, o_ref in enumerate(o_refs):
        o_ref[...] = (h + xx * mu_ref[j:j + 1, :]).astype(o_ref.dtype)


def shiftmix(x, g, mod3, mu, seqs, *, tm=256):
    t, d = x.shape
    nb8 = t // TILE_SUBLANES
    r8 = tm // TILE_SUBLANES
    return pl.pallas_call(
        functools.partial(_shiftmix_kernel, seqs=seqs, tm=tm),
        grid=(t // tm,),
        in_specs=[pl.BlockSpec((tm, d), lambda i: (i, 0)),
                  pl.BlockSpec((TILE_SUBLANES, d), lambda i: (jnp.maximum(i * r8 - 1, 0), 0)),
                  pl.BlockSpec((TILE_SUBLANES, d), lambda i: (jnp.minimum((i + 1) * r8, nb8 - 1), 0)),
                  pl.BlockSpec((1, d), lambda i: (0, 0)),
                  pl.BlockSpec((None, 6, d), lambda i: (seqs.seq_of_tile(i, tm), 0, 0)),
                  pl.BlockSpec((6, d), lambda i: (0, 0))],
        out_specs=[pl.BlockSpec((tm, d), lambda i: (i, 0))] * 6,
        out_shape=[jax.ShapeDtypeStruct((t, d), BF16)] * 6,
        name="rwkv_shiftmix",
        compiler_params=_params("parallel"),
    )(x, x, x, g, mod3, mu)


def _cast_kernel(w_ref, o_ref):
    o_ref[...] = w_ref[...].astype(o_ref.dtype)


def cast_bf16(w, layer):
    _, e, k, n = w.shape
    return pl.pallas_call(
        _cast_kernel,
        grid=(e,),
        in_specs=[pl.BlockSpec((None, None, k, n), lambda i: (layer, i, 0, 0))],
        out_specs=pl.BlockSpec((None, k, n), lambda i: (i, 0, 0)),
        out_shape=jax.ShapeDtypeStruct((e, k, n), BF16),
        name="cast_bf16",
        compiler_params=_params("parallel"),
    )(w)


def _mm_kernel(a_ref, w_ref, *refs, epi, tn, n_extra, precision):
    extra = refs[:n_extra]
    o_ref = refs[n_extra]
    n = o_ref.shape[-1]
    a = a_ref[...]
    for c in range(n // tn):
        sl = slice(c * tn, (c + 1) * tn)
        acc = jnp.dot(a, w_ref[:, sl], preferred_element_type=F32, precision=precision)
        if epi is not None:
            acc = epi(acc, *[e[:, sl] for e in extra])
        o_ref[:, sl] = acc.astype(o_ref.dtype)


def mm(a, w, *, out_dtype, name, epi=None, extras=(), seqs=None, tm=512, tn=512, nb=None, precision=None):
    m, k = a.shape
    n = w.shape[1]
    nb = n if nb is None else nb
    tn = max([c for c in range(TILE_LANES, min(tn, nb) + 1, TILE_LANES) if nb % c == 0], default=nb)
    tm = min(tm, m)
    specs = [pl.BlockSpec((tm, k), lambda i, j: (i, 0)), pl.BlockSpec((k, nb), lambda i, j: (0, j))]
    args = [a, w]
    for arr, kind in extras:
        if kind == "col":
            specs.append(pl.BlockSpec((1, nb), lambda i, j: (0, j)))
        elif kind == "row":
            specs.append(pl.BlockSpec((tm, nb), lambda i, j: (i, j)))
        else:
            specs.append(pl.BlockSpec((None, arr.shape[1], nb),
                                      lambda i, j: (seqs.seq_of_tile(i, tm), 0, j)))
        args.append(arr)
    return pl.pallas_call(
        functools.partial(_mm_kernel, epi=epi, tn=tn, n_extra=len(extras), precision=precision),
        grid=(m // tm, n // nb),
        in_specs=specs,
        out_specs=pl.BlockSpec((tm, nb), lambda i, j: (i, j)),
        out_shape=jax.ShapeDtypeStruct((m, n), out_dtype),
        name=name,
        compiler_params=_params("parallel", "arbitrary"),
    )(*args)


def _epi_bias(acc, b):
    return acc + b


def _epi_tanh(acc):
    return jnp.tanh(acc)


def _epi_sigmoid(acc):
    return jax.nn.sigmoid(acc)


def _epi_log_decay(acc, w0):
    y = -(acc + w0)
    softplus = jnp.maximum(y, 0.0) + jnp.log1p(jnp.exp(-jnp.abs(y)))
    return -jnp.exp(-softplus - 0.5)


def _epi_bias_sigmoid(acc, b):
    return jax.nn.sigmoid(acc + b)


def _epi_residual(row):
    def epi(acc, x, mod):
        return x + mod[row:row + 1, :] * acc
    return epi


def _expand_bd(m, bm):
    mb = m.astype(BF16)
    return jnp.where(bm, jnp.concatenate([mb] * (GROUP_LANES // HEAD), axis=0), jnp.zeros((), BF16))


def _wkv_chunks(chains, bm, ones_bd):
    c = CHUNK
    n = len(chains)
    rev = [ch[8] for ch in chains]
    t_i = lax.broadcasted_iota(jnp.int32, (c, c), 0)
    s_i = lax.broadcasted_iota(jnp.int32, (c, c), 1)
    tri = {False: (s_i <= t_i).astype(BF16), True: (s_i >= t_i).astype(BF16)}
    t2 = lax.broadcasted_iota(jnp.int32, (c, GROUP_LANES), 0)
    s2 = lax.broadcasted_iota(jnp.int32, (c, GROUP_LANES), 1) % c
    strict = {False: s2 < t2, True: s2 > t2}
    incl = {False: s2 <= t2, True: s2 >= t2}
    eye = jnp.where(s2 == t2, 1.0, 0.0)

    k2 = [ch[1] * ch[5] for ch in chains]
    ss = [_group_sum(q * q, ones_bd) for q in k2]
    kk = [q / jnp.maximum(jnp.sqrt(s), 1e-12) for q, s in zip(k2, ss)]
    kd = [ch[1] * (1.0 + (ch[4] - 1.0) * ch[6]) for ch in chains]
    split = [_split_bf16(ch[3]) for ch in chains]
    cw = [_dot(tri[rv], hi) + _dot(tri[rv], lo) for (hi, lo), rv in zip(split, rev)]
    edge = [w[0:1, :] if rv else w[c - 1:c, :] for w, rv in zip(cw, rev)]
    e_neg = [jnp.exp(-w) for w in cw]
    al = [-q * jnp.exp(w - ch[3]) for q, w, ch in zip(kk, cw, chains)]
    rb = [ch[0] * jnp.exp(w) for w, ch in zip(cw, chains)]
    ka = [q * ch[4] for q, ch in zip(kk, chains)]
    be = [q * e for q, e in zip(ka, e_neg)]
    kb = [q * e for q, e in zip(kd, e_neg)]
    x = [jnp.concatenate([p, q], axis=0).astype(BF16) for p, q in zip(al, rb)]
    z_bd = [jnp.concatenate([_expand_bd(p, bm), _expand_bd(q, bm)], axis=0) for p, q in zip(be, kb)]
    a_all = [_dot_nt(p, q) for p, q in zip(x, z_bd)]
    a_ab = [jnp.where(strict[rv], m[:c, :GROUP_LANES], 0.0) for m, rv in zip(a_all, rev)]
    a_kr = [jnp.concatenate([jnp.where(strict[rv], m[:c, GROUP_LANES:], 0.0),
                             jnp.where(incl[rv], m[c:, GROUP_LANES:], 0.0)], axis=0).astype(BF16)
            for m, rv in zip(a_all, rev)]
    a_rb = [jnp.where(incl[rv], m[c:, :GROUP_LANES], 0.0).astype(BF16) for m, rv in zip(a_all, rev)]
    p = [eye + m for m in a_ab]
    ap = [_dot(m.astype(BF16), _expand_bd(m, bm)) for m in a_ab]
    for _ in range(4):
        both = [_dot(jnp.concatenate([m, q], axis=0).astype(BF16), _expand_bd(m, bm)) for m, q in zip(ap, p)]
        p = [q + b[c:] for q, b in zip(p, both)]
        ap = [b[:c] for b in both]
    p = [q + _dot(q.astype(BF16), _expand_bd(m, bm)) for q, m in zip(p, ap)]
    xh = [_dot_nt(q, ch[7].astype(BF16)) for q, ch in zip(x, chains)]
    av = [_dot(m, _expand_bd(ch[2], bm)) for m, ch in zip(a_kr, chains)]
    u = [_dot(q.astype(BF16), _expand_bd(h[:c] + w[:c], bm)) for q, h, w in zip(p, xh, av)]
    y = [h[c:] + w[c:] + _dot(m, _expand_bd(q, bm)) for h, w, m, q in zip(xh, av, a_rb, u)]
    e_rest = [jnp.exp(e - w) for e, w in zip(edge, cw)]
    zw = [jnp.concatenate([p_ * e, q * e], axis=0).astype(BF16) for p_, q, e in zip(ka, kd, e_rest)]
    uv = [jnp.concatenate([q, ch[2]], axis=0).astype(BF16) for q, ch in zip(u, chains)]
    upd = [_dot_tn(p_, q) for p_, q in zip(uv, zw)]
    ht_new = [chains[j][7] * jnp.exp(edge[j]) + jnp.where(bm, upd[j], 0.0) for j in range(n)]
    return y, ht_new


def _wkv_kernel(rf, kf, vf, lwf, af, rb, kb, vb, lwb, ab, kk_ref, ka_ref, yf_ref, yb_ref, st_ref,
                *, seqs, ct, ngb):
    i = pl.program_id(1)
    nt = pl.num_programs(1)
    nc = ct // CHUNK
    bm = _block_mask()
    ones_bd = bm.astype(BF16)

    def load(refs, lanes, off, st_idx, reset, reverse):
        rows = pl.ds(off, CHUNK)
        r, k, v, lw, a = [ref[rows, lanes] for ref in refs]
        ht = jnp.where(reset, 0.0, st_ref[st_idx])
        return (r, k, v, lw, a, kk_ref[:, lanes], ka_ref[:, lanes], ht, reverse)

    def body(c, carry):
        off_f = pl.multiple_of(c * CHUNK, CHUNK)
        off_b = pl.multiple_of((nc - 1 - c) * CHUNK, CHUNK)
        tok_b = (nt - 1 - i) * ct + off_b
        reset_f = seqs.starts_seq(i * ct + off_f)
        reset_b = seqs.ends_seq(tok_b + CHUNK)
        lanes = [slice(g * GROUP_LANES, (g + 1) * GROUP_LANES) for g in range(ngb)]
        chains = [load((rf, kf, vf, lwf, af), lanes[g], off_f, g, reset_f, False) for g in range(ngb)]
        chains += [load((rb, kb, vb, lwb, ab), lanes[g], off_b, ngb + g, reset_b, True) for g in range(ngb)]
        y, ht = _wkv_chunks(chains, bm, ones_bd)
        for g in range(ngb):
            st_ref[g] = ht[g]
            st_ref[ngb + g] = ht[ngb + g]
            yf_ref[pl.ds(off_f, CHUNK), lanes[g]] = y[g]
            yb_ref[pl.ds(off_b, CHUNK), lanes[g]] = y[ngb + g]
        return carry

    lax.fori_loop(0, nc, body, 0)


def wkv(r, k, v, lw2, a2, k_k, k_a, seqs, *, ct=256, ngb=4):
    t, d = r.shape
    ngb = min(ngb, d // GROUP_LANES)
    wl = ngb * GROUP_LANES
    nlb = d // wl
    nt = t // ct
    fwd = lambda g, i: (i, g)
    bwd = lambda g, i: (nt - 1 - i, g)
    bwd2 = lambda g, i: (nt - 1 - i, nlb + g)
    blk = (ct, wl)
    par = pl.BlockSpec((1, wl), lambda g, i: (0, g))
    return pl.pallas_call(
        functools.partial(_wkv_kernel, seqs=seqs, ct=ct, ngb=ngb),
        grid=(nlb, nt),
        in_specs=[pl.BlockSpec(blk, fwd)] * 5 + [pl.BlockSpec(blk, bwd)] * 3 + [pl.BlockSpec(blk, bwd2)] * 2
                 + [par, par],
        out_specs=[pl.BlockSpec(blk, fwd), pl.BlockSpec(blk, bwd)],
        out_shape=[jax.ShapeDtypeStruct((t, d), F32)] * 2,
        scratch_shapes=[pltpu.VMEM((2 * ngb, GROUP_LANES, GROUP_LANES), F32)],
        name="wkv_scan",
        compiler_params=_params("parallel", "arbitrary"),
    )(r, k, v, lw2, a2, r, k, v, lw2, a2, k_k, k_a)


def _rwkv_post_kernel(yf, yb, r, k, v, af, ab, g, ka, rk, gw, gb, o_ref):
    ones_bd = _block_mask().astype(BF16)
    inv_n = 1.0 / HEAD
    y = yf[...] + yb[...]
    mean = _group_sum(y, ones_bd) * inv_n
    yc = y - mean
    var = _group_sum(yc * yc, ones_bd) * inv_n
    yn = yc * lax.rsqrt(var + GN_EPS) * gw[...] + gb[...]
    kv = k[...]
    k_a = ka[...]
    kd_sum = kv * (1.0 + (af[...] - 1.0) * k_a) + kv * (1.0 + (ab[...] - 1.0) * k_a)
    bonus = _group_sum(r[...] * kd_sum * rk[...], ones_bd) * v[...]
    o_ref[...] = ((yn + bonus) * g[...]).astype(o_ref.dtype)


def rwkv_post(yf, yb, r, k, v, a2, g, k_a, r_k, gn_w, gn_b, *, tm=512):
    t, d = r.shape
    ng = d // GROUP_LANES
    blk = pl.BlockSpec((tm, GROUP_LANES), lambda i, j: (i, j))
    blk_b = pl.BlockSpec((tm, GROUP_LANES), lambda i, j: (i, ng + j))
    par = pl.BlockSpec((1, GROUP_LANES), lambda i, j: (0, j))
    return pl.pallas_call(
        _rwkv_post_kernel,
        grid=(t // tm, ng),
        in_specs=[blk] * 5 + [blk, blk_b, blk] + [par] * 4,
        out_specs=blk,
        out_shape=jax.ShapeDtypeStruct((t, d), BF16),
        name="rwkv_post",
        compiler_params=_params("parallel", "parallel"),
    )(yf, yb, r, k, v, a2, a2, g, k_a, r_k, gn_w, gn_b)


def _attn_kernel(sink_ref, q_ref, kp_ref, kc_ref, kn_ref, vp_ref, vc_ref, vn_ref, o_ref, *, seqs, n_heads):
    i = pl.program_id(0)
    blk = ATT_BLOCK
    span = blk + 2 * WINDOW
    first = seqs.starts_seq(i * blk)
    last = seqs.ends_seq((i + 1) * blk)
    qi = lax.broadcasted_iota(jnp.int32, (blk, span), 0)
    ci = lax.broadcasted_iota(jnp.int32, (blk, span), 1)
    dist_i = jnp.abs(qi - (ci - WINDOW))
    valid = (dist_i <= WINDOW) & ((ci >= WINDOW) | jnp.logical_not(first)) \
        & ((ci < WINDOW + blk) | jnp.logical_not(last))
    dist = dist_i.astype(F32)
    n_kv = n_heads // ATT_GROUP
    kv_batch = min(ATT_KV_BATCH, n_kv)
    outs = []
    for kv0 in range(0, n_kv, kv_batch):
        heads = [(kvh, kvh * ATT_GROUP + g) for kvh in range(kv0, kv0 + kv_batch) for g in range(ATT_GROUP)]
        k_span, v_span = {}, {}
        for kvh in range(kv0, kv0 + kv_batch):
            ks = slice(kvh * HEAD, (kvh + 1) * HEAD)
            k_span[kvh] = jnp.concatenate([kp_ref[:, ks], kc_ref[:, ks], kn_ref[:, ks]], axis=0)
            v_span[kvh] = jnp.concatenate([vp_ref[:, ks], vc_ref[:, ks], vn_ref[:, ks]], axis=0)
        s = [_dot_nt(q_ref[:, h * HEAD:(h + 1) * HEAD], k_span[kvh]) for kvh, h in heads]
        s = [jnp.where(valid, sc * (HEAD ** -0.5) - (2.0 ** (-8.0 * (h + 1) / n_heads)) * dist, NEG_INF)
             for sc, (_, h) in zip(s, heads)]
        m = [jnp.maximum(jnp.max(sc, axis=-1, keepdims=True), sink_ref[h]) for sc, (_, h) in zip(s, heads)]
        e = [jnp.exp(sc - mx) for sc, mx in zip(s, m)]
        denom = [jnp.sum(ex, axis=-1, keepdims=True) + jnp.exp(sink_ref[h] - mx)
                 for ex, mx, (_, h) in zip(e, m, heads)]
        pv = [_dot(ex.astype(BF16), v_span[kvh]) for ex, (kvh, _) in zip(e, heads)]
        outs += [o / dn for o, dn in zip(pv, denom)]
    o_ref[...] = jnp.concatenate(outs, axis=-1).astype(o_ref.dtype)


def window_attention(qkv, sink, seqs, *, d):
    t = qkv.shape[0]
    n_heads = d // HEAD
    kvw = (n_heads // ATT_GROUP) * HEAD
    nb = t // ATT_BLOCK
    kcol = d // kvw
    prev = lambda i: jnp.maximum(i - 1, 0)
    nxt = lambda i: jnp.minimum(i + 1, nb - 1)
    kv_specs = [pl.BlockSpec((ATT_BLOCK, kvw), lambda i, c=c, f=f: (f(i), c))
                for c in (kcol, kcol + 1) for f in (prev, lambda i: i, nxt)]
    return pl.pallas_call(
        functools.partial(_attn_kernel, seqs=seqs, n_heads=n_heads),
        grid=(nb,),
        in_specs=[pl.BlockSpec(memory_space=pltpu.SMEM),
                  pl.BlockSpec((ATT_BLOCK, d), lambda i: (i, 0))] + kv_specs,
        out_specs=pl.BlockSpec((ATT_BLOCK, d), lambda i: (i, 0)),
        out_shape=jax.ShapeDtypeStruct((t, d), BF16),
        name="window_attention",
        compiler_params=_params("parallel"),
    )(sink, qkv, qkv, qkv, qkv, qkv, qkv, qkv)


def _pack_rows(ref, y):
    m = y.shape[0]
    half = y.shape[1] // 2
    ns = half // TILE_LANES
    lo = pltpu.bitcast(y[:, :half].astype(BF16).astype(F32), jnp.uint32) >> 16
    hi = pltpu.bitcast(y[:, half:].astype(BF16).astype(F32), jnp.uint32) & jnp.uint32(0xFFFF0000)
    w = lo | hi
    for s in range(ns):
        ref[pl.ds(s, m, stride=ns), :] = w[:, s * TILE_LANES:(s + 1) * TILE_LANES]


def _unpack_rows(ref, s, m, ns):
    w = ref[pl.ds(s, m, stride=ns), :]
    lo = pltpu.bitcast(w << 16, F32)
    hi = pltpu.bitcast(w & jnp.uint32(0xFFFF0000), F32)
    return lo, hi


def _router_kernel(x_ref, g_ref, mod_ref, rw_ref, rb_ref, hp_ref, idx_ref, gate_ref, pos_ref, cnt_ref,
                   run_ref, *, tm):
    i = pl.program_id(0)

    @pl.when(i == 0)
    def _():
        run_ref[...] = jnp.zeros_like(run_ref)

    h = _modnorm(x_ref[...], g_ref[...], mod_ref[3:4, :], mod_ref[4:5, :])
    _pack_rows(hp_ref, h)
    logits = jnp.dot(h, rw_ref[...], preferred_element_type=F32, precision=lax.Precision.HIGHEST)
    scores = jax.nn.sigmoid(logits)
    n_e = scores.shape[1]
    lane = lax.broadcasted_iota(jnp.int32, scores.shape, 1).astype(F32)
    work = scores + rb_ref[...]
    sel = jnp.zeros_like(scores)
    hot, idx, gates = [], [], []
    for _ in range(TOP_K):
        m = jnp.max(work, axis=-1, keepdims=True)
        first = jnp.min(jnp.where(work == m, lane, float(n_e)), axis=-1, keepdims=True)
        one = lane == first
        work = jnp.where(one, -jnp.inf, work)
        hot.append(one)
        idx.append(first)
        gates.append(jnp.sum(jnp.where(one, scores, 0.0), axis=-1, keepdims=True))
        sel = sel + one.astype(F32)
    total = gates[0]
    for gj in gates[1:]:
        total = total + gj
    t_i = lax.broadcasted_iota(jnp.int32, (tm, tm), 0)
    s_i = lax.broadcasted_iota(jnp.int32, (tm, tm), 1)
    before = _dot((s_i < t_i).astype(BF16), sel.astype(BF16)) + run_ref[...]
    run_ref[...] = run_ref[...] + jnp.sum(sel, axis=0, keepdims=True)
    cnt_ref[...] = run_ref[...]
    slot = lax.broadcasted_iota(jnp.int32, (tm, SLOT), 1)
    idx_o = jnp.zeros((tm, SLOT), jnp.int32)
    gate_o = jnp.zeros((tm, SLOT), F32)
    pos_o = jnp.zeros((tm, SLOT), jnp.int32)
    for j in range(TOP_K):
        pos_j = jnp.sum(jnp.where(hot[j], before, 0.0), axis=-1, keepdims=True).astype(jnp.int32)
        idx_o = jnp.where(slot == j, idx[j].astype(jnp.int32), idx_o)
        gate_o = jnp.where(slot == j, gates[j] / total * ROUTED_SCALE, gate_o)
        pos_o = jnp.where(slot == j, pos_j, pos_o)
    idx_ref[...] = idx_o
    gate_ref[...] = gate_o
    pos_ref[...] = pos_o


def router(x, g, mod3, router_w, router_bias, seqs, *, tm=256):
    t, d = x.shape
    n_e = router_w.shape[1]
    ns = d // (2 * TILE_LANES)
    tok = lambda i: (i, 0)
    return pl.pallas_call(
        functools.partial(_router_kernel, tm=tm),
        grid=(t // tm,),
        in_specs=[pl.BlockSpec((tm, d), tok),
                  pl.BlockSpec((1, d), lambda i: (0, 0)),
                  pl.BlockSpec((None, 6, d), lambda i: (seqs.seq_of_tile(i, tm), 0, 0)),
                  pl.BlockSpec((d, n_e), lambda i: (0, 0)),
                  pl.BlockSpec((1, n_e), lambda i: (0, 0))],
        out_specs=[pl.BlockSpec((tm * ns, TILE_LANES), tok),
                   pl.BlockSpec((tm, SLOT), tok), pl.BlockSpec((tm, SLOT), tok), pl.BlockSpec((tm, SLOT), tok),
                   pl.BlockSpec((1, n_e), lambda i: (0, 0))],
        out_shape=[jax.ShapeDtypeStruct((t * ns, TILE_LANES), jnp.uint32),
                   jax.ShapeDtypeStruct((t, SLOT), jnp.int32),
                   jax.ShapeDtypeStruct((t, SLOT), F32),
                   jax.ShapeDtypeStruct((t, SLOT), jnp.int32),
                   jax.ShapeDtypeStruct((1, n_e), F32)],
        scratch_shapes=[pltpu.VMEM((1, n_e), F32)],
        name="moe_router",
        compiler_params=_params("arbitrary"),
    )(x, g, mod3, router_w, router_bias)


def _dispatch_kernel(dest_ref, hp_ref, zeros_ref, xs_ref, sem, *, tm, ns):
    del zeros_ref

    def row_copy(t, j):
        src = hp_ref.at[pl.ds(pl.multiple_of(t * ns, ns), ns)]
        dst = xs_ref.at[pl.ds(pl.multiple_of(dest_ref[t * SLOT + j], ns), ns)]
        return pltpu.make_async_copy(src, dst, sem)

    def issue(t, carry):
        for j in range(TOP_K):
            row_copy(t, j).start()
        return carry

    def drain(t, carry):
        for j in range(TOP_K):
            row_copy(t, j).wait()
        return carry

    lax.fori_loop(0, tm, issue, 0, unroll=DMA_LOOP_UNROLL)
    lax.fori_loop(0, tm, drain, 0, unroll=DMA_LOOP_UNROLL)


def dispatch(hp, dest_rows, n_rows, ns, *, tm=256):
    t = hp.shape[0] // ns
    zeros = jnp.zeros((n_rows * ns, TILE_LANES), jnp.uint32)
    return pl.pallas_call(
        functools.partial(_dispatch_kernel, tm=tm, ns=ns),
        grid=(t // tm,),
        in_specs=[pl.BlockSpec((tm * SLOT,), lambda i: (i,), memory_space=pltpu.SMEM),
                  pl.BlockSpec((tm * ns, TILE_LANES), lambda i: (i, 0)),
                  pl.BlockSpec(memory_space=pl.ANY)],
        out_specs=pl.BlockSpec(memory_space=pl.ANY),
        out_shape=jax.ShapeDtypeStruct((n_rows * ns, TILE_LANES), jnp.uint32),
        scratch_shapes=[pltpu.SemaphoreType.DMA(())],
        input_output_aliases={2: 0},
        name="moe_dispatch",
        compiler_params=_params("arbitrary"),
    )(dest_rows, hp, zeros)


def _expert_kernel(be_ref, nblk_ref, xs_ref, wg_ref, wu_ref, wd_ref, o_ref, xb_ref, *, packed_out):
    i = pl.program_id(0)
    bm = xb_ref.shape[0]
    half = xb_ref.shape[1] // 2
    ns = half // TILE_LANES

    @pl.when(i < nblk_ref[0])
    def _():
        for s in range(ns):
            lo, hi = _unpack_rows(xs_ref, s, bm, ns)
            xb_ref[:, s * TILE_LANES:(s + 1) * TILE_LANES] = lo.astype(BF16)
            xb_ref[:, half + s * TILE_LANES:half + (s + 1) * TILE_LANES] = hi.astype(BF16)
        xb = xb_ref[...]
        hg = _dot(xb, wg_ref[...])
        hu = _dot(xb, wu_ref[...])
        hb = (hg * jax.nn.sigmoid(hg) * hu).astype(BF16)
        y = _dot(hb, wd_ref[...])
        if packed_out:
            _pack_rows(o_ref, y)
        else:
            o_ref[...] = y

    @pl.when(i >= nblk_ref[0])
    def _():
        o_ref[...] = jnp.zeros_like(o_ref)


def expert_ffn(xs, blk_expert, n_used, w_gate, w_up, w_down, *, bm, packed_out, name):
    d, f = w_gate.shape[1:]
    ns = d // (2 * TILE_LANES)
    n_rows = xs.shape[0] // ns
    nblk = n_rows // bm
    if packed_out:
        o_spec = pl.BlockSpec((bm * ns, TILE_LANES), lambda i, be, nb: (i, 0))
        o_shape = jax.ShapeDtypeStruct((n_rows * ns, TILE_LANES), jnp.uint32)
    else:
        o_spec = pl.BlockSpec((bm, d), lambda i, be, nb: (i, 0))
        o_shape = jax.ShapeDtypeStruct((n_rows, d), F32)
    return pl.pallas_call(
        functools.partial(_expert_kernel, packed_out=packed_out),
        grid_spec=pltpu.PrefetchScalarGridSpec(
            num_scalar_prefetch=2,
            grid=(nblk,),
            in_specs=[pl.BlockSpec((bm * ns, TILE_LANES), lambda i, be, nb: (i, 0)),
                      pl.BlockSpec((None, d, f), lambda i, be, nb: (be[i], 0, 0)),
                      pl.BlockSpec((None, d, f), lambda i, be, nb: (be[i], 0, 0)),
                      pl.BlockSpec((None, f, d), lambda i, be, nb: (be[i], 0, 0))],
            out_specs=o_spec,
            scratch_shapes=[pltpu.VMEM((bm, d), BF16)]),
        out_shape=o_shape,
        name=name,
        compiler_params=_params("arbitrary"),
    )(blk_expert, n_used, xs, w_gate, w_up, w_down)


def _combine_kernel(dest_ref, dest_next_ref, ys_ref, gate_ref, sh_ref, x_ref, mod_ref, o_ref, buf_ref, sems,
                    *, tm, ns):
    i = pl.program_id(0)
    n = pl.num_programs(0)
    slot = lax.rem(i, 2)

    def row_copy(dref, t, j, sl):
        src = ys_ref.at[pl.ds(pl.multiple_of(dref[t * SLOT + j], ns), ns)]
        dst = buf_ref.at[sl * TOP_K + j, pl.ds(pl.multiple_of(t * ns, ns), ns)]
        return pltpu.make_async_copy(src, dst, sems.at[sl])

    def gather(dref, sl):
        def issue(t, carry):
            for j in range(TOP_K):
                row_copy(dref, t, j, sl).start()
            return carry
        lax.fori_loop(0, tm, issue, 0, unroll=DMA_LOOP_UNROLL)

    @pl.when(i == 0)
    def _():
        gather(dest_ref, 0)

    @pl.when(i + 1 < n)
    def _():
        gather(dest_next_ref, 1 - slot)

    def drain(t, carry):
        for j in range(TOP_K):
            row_copy(dest_ref, t, j, slot).wait()
        return carry

    lax.fori_loop(0, tm, drain, 0, unroll=DMA_LOOP_UNROLL)
    gates = gate_ref[...]
    gj = [jnp.broadcast_to(gates[:, j:j + 1], (tm, TILE_LANES)) for j in range(TOP_K)]
    gate_f = mod_ref[5:6, :]
    half = x_ref.shape[1] // 2
    for s in range(ns):
        acc_lo = jnp.zeros((tm, TILE_LANES), F32)
        acc_hi = jnp.zeros((tm, TILE_LANES), F32)
        for j in range(TOP_K):
            lo, hi = _unpack_rows(buf_ref.at[slot * TOP_K + j], s, tm, ns)
            acc_lo = acc_lo + gj[j] * lo
            acc_hi = acc_hi + gj[j] * hi
        for acc, off in ((acc_lo, s * TILE_LANES), (acc_hi, half + s * TILE_LANES)):
            sl = slice(off, off + TILE_LANES)
            o_ref[:, sl] = x_ref[:, sl] + gate_f[:, sl] * (acc + sh_ref[:, sl])


def combine(ys, dest_rows, gates, shared, x, mod3, seqs, *, tm=128):
    t, d = x.shape
    ns = d // (2 * TILE_LANES)
    n = t // tm
    tok = lambda i: (i, 0)
    return pl.pallas_call(
        functools.partial(_combine_kernel, tm=tm, ns=ns),
        grid=(n,),
        in_specs=[pl.BlockSpec((tm * SLOT,), lambda i: (i,), memory_space=pltpu.SMEM),
                  pl.BlockSpec((tm * SLOT,), lambda i: (jnp.minimum(i + 1, n - 1),), memory_space=pltpu.SMEM),
                  pl.BlockSpec(memory_space=pl.ANY),
                  pl.BlockSpec((tm, SLOT), tok),
                  pl.BlockSpec((tm, d), tok),
                  pl.BlockSpec((tm, d), tok),
                  pl.BlockSpec((None, 6, d), lambda i: (seqs.seq_of_tile(i, tm), 0, 0))],
        out_specs=pl.BlockSpec((tm, d), tok),
        out_shape=jax.ShapeDtypeStruct((t, d), F32),
        scratch_shapes=[pltpu.VMEM((2 * TOP_K, tm * ns, TILE_LANES), jnp.uint32),
                        pltpu.SemaphoreType.DMA((2,))],
        name="moe_combine",
        compiler_params=_params("arbitrary"),
    )(dest_rows, dest_rows, ys, gates, shared, x, mod3)


def moe_block(x, mod3, g, router_w, router_bias, w_gate, w_up, w_down, sh_gate, sh_up, sh_down, layer, seqs,
              *, bm=256):
    t, d = x.shape
    n_e = router_w.shape[1]
    hp, idx, gates, pos, counts = router(x, g, mod3, router_w, router_bias, seqs)
    cnt = counts[0].astype(jnp.int32)
    padded = (cnt + bm - 1) // bm * bm
    end_pad = jnp.cumsum(padded)
    start_pad = end_pad - padded
    ns = d // (2 * TILE_LANES)
    dest_rows = ((jnp.take(start_pad, idx, axis=0) + pos) * ns).reshape(-1)
    nblk = -(-(t * TOP_K) // bm) + n_e
    blk_start = jnp.arange(nblk, dtype=jnp.int32) * bm
    blk_expert = jnp.minimum(jnp.sum(end_pad[None, :] <= blk_start[:, None], axis=1), n_e - 1).astype(jnp.int32)
    n_used = (end_pad[-1:] // bm).astype(jnp.int32)
    xs = dispatch(hp, dest_rows, nblk * bm, ns)
    bf = lambda w: cast_bf16(w, layer)
    ys = expert_ffn(xs, blk_expert, n_used, bf(w_gate), bf(w_up), bf(w_down), bm=bm, packed_out=True,
                    name="moe_experts")
    sh = lambda w: cast_bf16(w[:, None], layer)
    shared = expert_ffn(hp, jnp.zeros((t // bm,), jnp.int32), jnp.full((1,), t // bm, jnp.int32),
                        sh(sh_gate), sh(sh_up), sh(sh_down), bm=bm, packed_out=False, name="moe_shared_expert")
    return combine(ys, dest_rows, gates, shared, x, mod3, seqs)


def _final_norm_kernel(x_ref, g_ref, o_ref):
    x = x_ref[...]
    o_ref[...] = x * lax.rsqrt(jnp.mean(x * x, axis=-1, keepdims=True) + RMS_EPS) * g_ref[...]


def final_rmsnorm(x, g, row0, rows, *, tm=256):
    d = x.shape[1]
    b0 = row0 // tm
    return pl.pallas_call(
        _final_norm_kernel,
        grid=(rows // tm,),
        in_specs=[pl.BlockSpec((tm, d), lambda i: (b0 + i, 0)), pl.BlockSpec((1, d), lambda i: (0, 0))],
        out_specs=pl.BlockSpec((tm, d), lambda i: (i, 0)),
        out_shape=jax.ShapeDtypeStruct((rows, d), F32),
        name="final_norm",
        compiler_params=_params("parallel"),
    )(x, g)


def _block_diag2(w):
    k, n = w.shape[1:]
    z = jnp.zeros((k, n), w.dtype)
    return jnp.concatenate([jnp.concatenate([w[0], z], axis=1), jnp.concatenate([z, w[1]], axis=1)], axis=0)


def rwkv_block(x, mod3, g, seqs, mu, w_r, w_k, w_v, w_o, dec_w0, dec_w1, dec_w2, iclr_a0, iclr_a1, iclr_a2,
               gate_g1, gate_g2, k_k, k_a, r_k, gn_w, gn_b):
    d = x.shape[1]
    bf = lambda w: w.astype(BF16)
    xr, xw, xk, xv, xa, xg = shiftmix(x, g, mod3, mu, seqs)
    r = mm(xr, bf(w_r), out_dtype=F32, name="rwkv_proj")
    k = mm(xk, bf(w_k), out_dtype=F32, name="rwkv_proj")
    v = mm(xv, bf(w_v), out_dtype=F32, name="rwkv_proj")
    lw_in = mm(xw, bf(jnp.concatenate([dec_w1[0], dec_w1[1]], axis=1)), out_dtype=BF16, epi=_epi_tanh,
               name="rwkv_decay_lora_in")
    la_in = mm(xa, bf(jnp.concatenate([iclr_a1[0], iclr_a1[1]], axis=1)), out_dtype=BF16, name="rwkv_iclr_lora_in")
    lg_in = mm(xg, bf(gate_g1), out_dtype=BF16, epi=_epi_sigmoid, name="rwkv_gate_lora_in")
    gate = mm(lg_in, bf(gate_g2), out_dtype=F32, name="rwkv_gate_lora_out")
    lw2 = mm(lw_in, bf(_block_diag2(dec_w2)), out_dtype=F32, epi=_epi_log_decay,
             extras=[(dec_w0.reshape(1, 2 * d), "col")], name="rwkv_log_decay")
    a2 = mm(la_in, bf(_block_diag2(iclr_a2)), out_dtype=F32, epi=_epi_bias_sigmoid,
            extras=[(iclr_a0.reshape(1, 2 * d), "col")], name="rwkv_iclr")
    k_a2 = k_a.reshape(1, d)
    yf, yb = wkv(r, k, v, lw2, a2, k_k.reshape(1, d), k_a2, seqs)
    yo = rwkv_post(yf, yb, r, k, v, a2, gate, k_a2, r_k.reshape(1, d), gn_w.reshape(1, d), gn_b.reshape(1, d))
    return mm(yo, bf(w_o), out_dtype=F32, epi=_epi_residual(2), extras=[(x, "row"), (mod3, "seq")], seqs=seqs,
              name="mixer_out_proj")


def attention_block(x, mod3, g, seqs, w_qkv, w_o, sink):
    d = x.shape[1]
    h = modnorm(x, g, mod3, seqs, row=0)
    qkv = mm(h, w_qkv.astype(BF16), out_dtype=BF16, name="attn_qkv")
    o = window_attention(qkv, sink, seqs, d=d)
    return mm(o, w_o.astype(BF16), out_dtype=F32, epi=_epi_residual(2), extras=[(x, "row"), (mod3, "seq")],
              seqs=seqs, name="mixer_out_proj")


def kernel(x_prompt, x_sample, c_prompt, c_sample, ada_w, ada_b, norm_mix, norm_ffn, rw_mu, rw_wr, rw_wk, rw_wv, rw_wo, rw_dec_w0, rw_dec_w1, rw_dec_w2, rw_iclr_a0, rw_iclr_a1, rw_iclr_a2, rw_gate_g1, rw_gate_g2, rw_k_k, rw_k_a, rw_r_k, rw_gn_w, rw_gn_b, at_w_qkv, at_w_o, at_sink, router_w, router_bias, moe_w_gate, moe_w_up, moe_w_down, sh_w_gate, sh_w_up, sh_w_down, final_norm):
    b1, s1, d = x_prompt.shape
    b2, s2, _ = x_sample.shape
    seqs = Seqs(b1, s1, b2, s2)
    depth = ada_w.shape[0]
    x = jnp.concatenate([x_prompt.reshape(b1 * s1, d), x_sample.reshape(b2 * s2, d)], axis=0)
    c = jnp.concatenate([c_prompt, c_sample], axis=0)
    rows = -(-seqs.nseq // TILE_SUBLANES) * TILE_SUBLANES
    c_act = jnp.pad(jax.nn.silu(c), ((0, rows - seqs.nseq), (0, 0)))
    for i in range(depth):
        j = i // 2
        mod = mm(c_act, ada_w[i], out_dtype=F32, epi=_epi_bias, extras=[(ada_b[i].reshape(1, -1), "col")],
                 nb=d, precision=lax.Precision.HIGHEST, name="adaln_mod")
        mod3 = mod[:seqs.nseq].reshape(seqs.nseq, 6, d)
        g_mix = norm_mix[i].reshape(1, d)
        if i % 2 == 0:
            x = rwkv_block(x, mod3, g_mix, seqs, rw_mu[j], rw_wr[j], rw_wk[j], rw_wv[j], rw_wo[j], rw_dec_w0[j],
                           rw_dec_w1[j], rw_dec_w2[j], rw_iclr_a0[j], rw_iclr_a1[j], rw_iclr_a2[j],
                           rw_gate_g1[j], rw_gate_g2[j], rw_k_k[j], rw_k_a[j], rw_r_k[j], rw_gn_w[j], rw_gn_b[j])
        else:
            x = attention_block(x, mod3, g_mix, seqs, at_w_qkv[j], at_w_o[j], at_sink[j])
        x = moe_block(x, mod3, norm_ffn[i].reshape(1, d), router_w[i], router_bias[i].reshape(1, -1),
                      moe_w_gate, moe_w_up, moe_w_down, sh_w_gate, sh_w_up, sh_w_down, i, seqs)
    fw = final_norm.reshape(1, d)
    y1 = final_rmsnorm(x, fw, 0, seqs.t1).reshape(b1, s1, d)
    y2 = final_rmsnorm(x, fw, seqs.t1, b2 * s2).reshape(b2, s2, d)
    return (y1, y2)
```

```python
import functools
from typing import NamedTuple

import jax
import jax.numpy as jnp
from jax import lax
from jax.experimental import pallas as pl
from jax.experimental.pallas import tpu as pltpu

F32 = jnp.float32
BF16 = jnp.bfloat16

HEAD = 64
GROUP_LANES = 256
CHUNK = 64
ATT_GROUP = 4
ATT_KV_BATCH = 1
WINDOW = 128
ATT_BLOCK = 128
TOP_K = 6
ROUTED_SCALE = 2.446
RMS_EPS = 1e-6
GN_EPS = 64e-5
NEG_INF = -1e30
SLOT = 8
DMA_LOOP_UNROLL = 8
TILE_SUBLANES = 8
TILE_LANES = 128
VMEM_LIMIT = 56 * 1024 * 1024


class Seqs(NamedTuple):
    b1: int
    s1: int
    b2: int
    s2: int

    @property
    def t1(self):
        return self.b1 * self.s1

    @property
    def total(self):
        return self.b1 * self.s1 + self.b2 * self.s2

    @property
    def nseq(self):
        return self.b1 + self.b2

    def seq_of_tile(self, i, tm):
        n1 = self.t1 // tm
        return jnp.where(i < n1, i // (self.s1 // tm), self.b1 + (i - n1) // (self.s2 // tm))

    def starts_seq(self, tok):
        return jnp.where(tok < self.t1, lax.rem(tok, self.s1) == 0, lax.rem(tok - self.t1, self.s2) == 0)

    def ends_seq(self, tok_end):
        return jnp.where(tok_end <= self.t1, lax.rem(tok_end, self.s1) == 0,
                         lax.rem(tok_end - self.t1, self.s2) == 0)


def _params(*sem):
    return pltpu.CompilerParams(dimension_semantics=sem, vmem_limit_bytes=VMEM_LIMIT)


def _modnorm(x, g, shift, scale):
    y = x * lax.rsqrt(jnp.mean(x * x, axis=-1, keepdims=True) + RMS_EPS)
    return (y * g) * (1.0 + scale) + shift


def _split_bf16(x):
    hi = x.astype(BF16)
    lo = (x - hi.astype(F32)).astype(BF16)
    return hi, lo


def _dot(a, b):
    return jnp.dot(a, b, preferred_element_type=F32)


def _dot_nt(a, b):
    return lax.dot_general(a, b, (((1,), (1,)), ((), ())), preferred_element_type=F32)


def _dot_tn(a, b):
    return lax.dot_general(a, b, (((0,), (0,)), ((), ())), preferred_element_type=F32)


def _block_mask():
    r = lax.broadcasted_iota(jnp.int32, (GROUP_LANES, GROUP_LANES), 0) // HEAD
    c = lax.broadcasted_iota(jnp.int32, (GROUP_LANES, GROUP_LANES), 1) // HEAD
    return r == c


def _group_sum(x, ones_bd):
    hi, lo = _split_bf16(x)
    return _dot(hi, ones_bd) + _dot(lo, ones_bd)


def _modnorm_kernel(x_ref, g_ref, mod_ref, o_ref, *, row):
    h = _modnorm(x_ref[...], g_ref[...], mod_ref[row:row + 1, :], mod_ref[row + 1:row + 2, :])
    o_ref[...] = h.astype(o_ref.dtype)


def modnorm(x, g, mod3, seqs, *, row, tm=256):
    t, d = x.shape
    return pl.pallas_call(
        functools.partial(_modnorm_kernel, row=row),
        grid=(t // tm,),
        in_specs=[pl.BlockSpec((tm, d), lambda i: (i, 0)),
                  pl.BlockSpec((1, d), lambda i: (0, 0)),
                  pl.BlockSpec((None, 6, d), lambda i: (seqs.seq_of_tile(i, tm), 0, 0))],
        out_specs=pl.BlockSpec((tm, d), lambda i: (i, 0)),
        out_shape=jax.ShapeDtypeStruct((t, d), BF16),
        name="modnorm",
        compiler_params=_params("parallel"),
    )(x, g, mod3)


def _shiftmix_kernel(x_ref, xp_ref, xn_ref, g_ref, mod_ref, mu_ref, *o_refs, seqs, tm):
    i = pl.program_id(0)
    g = g_ref[...]
    shift = mod_ref[0:1, :]
    scale = mod_ref[1:2, :]
    h = _modnorm(x_ref[...], g, shift, scale)
    hp = _modnorm(xp_ref[...], g, shift, scale)[TILE_SUBLANES - 1:TILE_SUBLANES, :]
    hn = _modnorm(xn_ref[...], g, shift, scale)[0:1, :]
    hp = jnp.where(seqs.starts_seq(i * tm), 0.0, hp)
    hn = jnp.where(seqs.ends_seq((i + 1) * tm), 0.0, hn)
    rows = lax.broadcasted_iota(jnp.int32, h.shape, 0)
    prev = jnp.where(rows == 0, hp, pltpu.roll(h, 1, 0))
    nxt = jnp.where(rows == tm - 1, hn, pltpu.roll(h, tm - 1, 0))
    xx = 0.5 * (prev + nxt) - h
    for j, o_ref in enumerate(o_refs):
        o_ref[...] = (h + xx * mu_ref[j:j + 1, :]).astype(o_ref.dtype)


def shiftmix(x, g, mod3, mu, seqs, *, tm=256):
    t, d = x.shape
    nb8 = t // TILE_SUBLANES
    r8 = tm // TILE_SUBLANES
    return pl.pallas_call(
        functools.partial(_shiftmix_kernel, seqs=seqs, tm=tm),
        grid=(t // tm,),
        in_specs=[pl.BlockSpec((tm, d), lambda i: (i, 0)),
                  pl.BlockSpec((TILE_SUBLANES, d), lambda i: (jnp.maximum(i * r8 - 1, 0), 0)),
                  pl.BlockSpec((TILE_SUBLANES, d), lambda i: (jnp.minimum((i + 1) * r8, nb8 - 1), 0)),
                  pl.BlockSpec((1, d), lambda i: (0, 0)),
                  pl.BlockSpec((None, 6, d), lambda i: (seqs.seq_of_tile(i, tm), 0, 0)),
                  pl.BlockSpec((6, d), lambda i: (0, 0))],
        out_specs=[pl.BlockSpec((tm, d), lambda i: (i, 0))] * 6,
        out_shape=[jax.ShapeDtypeStruct((t, d), BF16)] * 6,
        name="rwkv_shiftmix",
        compiler_params=_params("parallel"),
    )(x, x, x, g, mod3, mu)


def _cast_kernel(w_ref, o_ref):
    o_ref[...] = w_ref[...].astype(o_ref.dtype)


def cast_bf16(w, layer):
    _, e, k, n = w.shape
    return pl.pallas_call(
        _cast_kernel,
        grid=(e,),
        in_specs=[pl.BlockSpec((None, None, k, n), lambda i: (layer, i, 0, 0))],
        out_specs=pl.BlockSpec((None, k, n), lambda i: (i, 0, 0)),
        out_shape=jax.ShapeDtypeStruct((e, k, n), BF16),
        name="cast_bf16",
        compiler_params=_params("parallel"),
    )(w)


def _mm_kernel(a_ref, w_ref, *refs, epi, tn, n_extra, precision):
    extra = refs[:n_extra]
    o_ref = refs[n_extra]
    n = o_ref.shape[-1]
    a = a_ref[...]
    for c in range(n // tn):
        sl = slice(c * tn, (c + 1) * tn)
        acc = jnp.dot(a, w_ref[:, sl], preferred_element_type=F32, precision=precision)
        if epi is not None:
            acc = epi(acc, *[e[:, sl] for e in extra])
        o_ref[:, sl] = acc.astype(o_ref.dtype)


def mm(a, w, *, out_dtype, name, epi=None, extras=(), seqs=None, tm=512, tn=512, nb=None, precision=None):
    m, k = a.shape
    n = w.shape[1]
    nb = n if nb is None else nb
    tn = max([c for c in range(TILE_LANES, min(tn, nb) + 1, TILE_LANES) if nb % c == 0], default=nb)
    tm = min(tm, m)
    specs = [pl.BlockSpec((tm, k), lambda i, j: (i, 0)), pl.BlockSpec((k, nb), lambda i, j: (0, j))]
    args = [a, w]
    for arr, kind in extras:
        if kind == "col":
            specs.append(pl.BlockSpec((1, nb), lambda i, j: (0, j)))
        elif kind == "row":
            specs.append(pl.BlockSpec((tm, nb), lambda i, j: (i, j)))
        else:
            specs.append(pl.BlockSpec((None, arr.shape[1], nb),
                                      lambda i, j: (seqs.seq_of_tile(i, tm), 0, j)))
        args.append(arr)
    return pl.pallas_call(
        functools.partial(_mm_kernel, epi=epi, tn=tn, n_extra=len(extras), precision=precision),
        grid=(m // tm, n // nb),
        in_specs=specs,
        out_specs=pl.BlockSpec((tm, nb), lambda i, j: (i, j)),
        out_shape=jax.ShapeDtypeStruct((m, n), out_dtype),
        name=name,
        compiler_params=_params("parallel", "arbitrary"),
    )(*args)


def _epi_bias(acc, b):
    return acc + b


def _epi_tanh(acc):
    return jnp.tanh(acc)


def _epi_sigmoid(acc):
    return jax.nn.sigmoid(acc)


def _epi_log_decay(acc, w0):
    y = -(acc + w0)
    softplus = jnp.maximum(y, 0.0) + jnp.log1p(jnp.exp(-jnp.abs(y)))
    return -jnp.exp(-softplus - 0.5)


def _epi_bias_sigmoid(acc, b):
    return jax.nn.sigmoid(acc + b)


def _epi_residual(row):
    def epi(acc, x, mod):
        return x + mod[row:row + 1, :] * acc
    return epi


def _expand_bd(m, bm):
    mb = m.astype(BF16)
    return jnp.where(bm, jnp.concatenate([mb] * (GROUP_LANES // HEAD), axis=0), jnp.zeros((), BF16))


def _wkv_chunks(chains, bm, ones_bd):
    c = CHUNK
    n = len(chains)
    rev = [ch[8] for ch in chains]
    t_i = lax.broadcasted_iota(jnp.int32, (c, c), 0)
    s_i = lax.broadcasted_iota(jnp.int32, (c, c), 1)
    tri = {False: (s_i <= t_i).astype(BF16), True: (s_i >= t_i).astype(BF16)}
    t2 = lax.broadcasted_iota(jnp.int32, (c, GROUP_LANES), 0)
    s2 = lax.broadcasted_iota(jnp.int32, (c, GROUP_LANES), 1) % c
    strict = {False: s2 < t2, True: s2 > t2}
    incl = {False: s2 <= t2, True: s2 >= t2}
    eye = jnp.where(s2 == t2, 1.0, 0.0)

    k2 = [ch[1] * ch[5] for ch in chains]
    ss = [_group_sum(q * q, ones_bd) for q in k2]
    kk = [q / jnp.maximum(jnp.sqrt(s), 1e-12) for q, s in zip(k2, ss)]
    kd = [ch[1] * (1.0 + (ch[4] - 1.0) * ch[6]) for ch in chains]
    split = [_split_bf16(ch[3]) for ch in chains]
    cw = [_dot(tri[rv], hi) + _dot(tri[rv], lo) for (hi, lo), rv in zip(split, rev)]
    edge = [w[0:1, :] if rv else w[c - 1:c, :] for w, rv in zip(cw, rev)]
    e_neg = [jnp.exp(-w) for w in cw]
    al = [-q * jnp.exp(w - ch[3]) for q, w, ch in zip(kk, cw, chains)]
    rb = [ch[0] * jnp.exp(w) for w, ch in zip(cw, chains)]
    ka = [q * ch[4] for q, ch in zip(kk, chains)]
    be = [q * e for q, e in zip(ka, e_neg)]
    kb = [q * e for q, e in zip(kd, e_neg)]
    x = [jnp.concatenate([p, q], axis=0).astype(BF16) for p, q in zip(al, rb)]
    z_bd = [jnp.concatenate([_expand_bd(p, bm), _expand_bd(q, bm)], axis=0) for p, q in zip(be, kb)]
    a_all = [_dot_nt(p, q) for p, q in zip(x, z_bd)]
    a_ab = [jnp.where(strict[rv], m[:c, :GROUP_LANES], 0.0) for m, rv in zip(a_all, rev)]
    a_kr = [jnp.concatenate([jnp.where(strict[rv], m[:c, GROUP_LANES:], 0.0),
                             jnp.where(incl[rv], m[c:, GROUP_LANES:], 0.0)], axis=0).astype(BF16)
            for m, rv in zip(a_all, rev)]
    a_rb = [jnp.where(incl[rv], m[c:, :GROUP_LANES], 0.0).astype(BF16) for m, rv in zip(a_all, rev)]
    p = [eye + m for m in a_ab]
    ap = [_dot(m.astype(BF16), _expand_bd(m, bm)) for m in a_ab]
    for _ in range(4):
        both = [_dot(jnp.concatenate([m, q], axis=0).astype(BF16), _expand_bd(m, bm)) for m, q in zip(ap, p)]
        p = [q + b[c:] for q, b in zip(p, both)]
        ap = [b[:c] for b in both]
    p = [q + _dot(q.astype(BF16), _expand_bd(m, bm)) for q, m in zip(p, ap)]
    xh = [_dot_nt(q, ch[7].astype(BF16)) for q, ch in zip(x, chains)]
    av = [_dot(m, _expand_bd(ch[2], bm)) for m, ch in zip(a_kr, chains)]
    u = [_dot(q.astype(BF16), _expand_bd(h[:c] + w[:c], bm)) for q, h, w in zip(p, xh, av)]
    y = [h[c:] + w[c:] + _dot(m, _expand_bd(q, bm)) for h, w, m, q in zip(xh, av, a_rb, u)]
    e_rest = [jnp.exp(e - w) for e, w in zip(edge, cw)]
    zw = [jnp.concatenate([p_ * e, q * e], axis=0).astype(BF16) for p_, q, e in zip(ka, kd, e_rest)]
    uv = [jnp.concatenate([q, ch[2]], axis=0).astype(BF16) for q, ch in zip(u, chains)]
    upd = [_dot_tn(p_, q) for p_, q in zip(uv, zw)]
    ht_new = [chains[j][7] * jnp.exp(edge[j]) + jnp.where(bm, upd[j], 0.0) for j in range(n)]
    return y, ht_new


def _wkv_kernel(rf, kf, vf, lwf, af, rb, kb, vb, lwb, ab, kk_ref, ka_ref, yf_ref, yb_ref, st_ref,
                *, seqs, ct, ngb):
    i = pl.program_id(1)
    nt = pl.num_programs(1)
    nc = ct // CHUNK
    bm = _block_mask()
    ones_bd = bm.astype(BF16)

    def load(refs, lanes, off, st_idx, reset, reverse):
        rows = pl.ds(off, CHUNK)
        r, k, v, lw, a = [ref[rows, lanes] for ref in refs]
        ht = jnp.where(reset, 0.0, st_ref[st_idx])
        return (r, k, v, lw, a, kk_ref[:, lanes], ka_ref[:, lanes], ht, reverse)

    def body(c, carry):
        off_f = pl.multiple_of(c * CHUNK, CHUNK)
        off_b = pl.multiple_of((nc - 1 - c) * CHUNK, CHUNK)
        tok_b = (nt - 1 - i) * ct + off_b
        reset_f = seqs.starts_seq(i * ct + off_f)
        reset_b = seqs.ends_seq(tok_b + CHUNK)
        lanes = [slice(g * GROUP_LANES, (g + 1) * GROUP_LANES) for g in range(ngb)]
        chains = [load((rf, kf, vf, lwf, af), lanes[g], off_f, g, reset_f, False) for g in range(ngb)]
        chains += [load((rb, kb, vb, lwb, ab), lanes[g], off_b, ngb + g, reset_b, True) for g in range(ngb)]
        y, ht = _wkv_chunks(chains, bm, ones_bd)
        for g in range(ngb):
            st_ref[g] = ht[g]
            st_ref[ngb + g] = ht[ngb + g]
            yf_ref[pl.ds(off_f, CHUNK), lanes[g]] = y[g]
            yb_ref[pl.ds(off_b, CHUNK), lanes[g]] = y[ngb + g]
        return carry

    lax.fori_loop(0, nc, body, 0)


def wkv(r, k, v, lw2, a2, k_k, k_a, seqs, *, ct=256, ngb=4):
    t, d = r.shape
    ngb = min(ngb, d // GROUP_LANES)
    wl = ngb * GROUP_LANES
    nlb = d // wl
    nt = t // ct
    fwd = lambda g, i: (i, g)
    bwd = lambda g, i: (nt - 1 - i, g)
    bwd2 = lambda g, i: (nt - 1 - i, nlb + g)
    blk = (ct, wl)
    par = pl.BlockSpec((1, wl), lambda g, i: (0, g))
    return pl.pallas_call(
        functools.partial(_wkv_kernel, seqs=seqs, ct=ct, ngb=ngb),
        grid=(nlb, nt),
        in_specs=[pl.BlockSpec(blk, fwd)] * 5 + [pl.BlockSpec(blk, bwd)] * 3 + [pl.BlockSpec(blk, bwd2)] * 2
                 + [par, par],
        out_specs=[pl.BlockSpec(blk, fwd), pl.BlockSpec(blk, bwd)],
        out_shape=[jax.ShapeDtypeStruct((t, d), F32)] * 2,
        scratch_shapes=[pltpu.VMEM((2 * ngb, GROUP_LANES, GROUP_LANES), F32)],
        name="wkv_scan",
        compiler_params=_params("parallel", "arbitrary"),
    )(r, k, v, lw2, a2, r, k, v, lw2, a2, k_k, k_a)


def _rwkv_post_kernel(yf, yb, r, k, v, af, ab, g, ka, rk, gw, gb, o_ref):
    ones_bd = _block_mask().astype(BF16)
    inv_n = 1.0 / HEAD
    y = yf[...] + yb[...]
    mean = _group_sum(y, ones_bd) * inv_n
    yc = y - mean
    var = _group_sum(yc * yc, ones_bd) * inv_n
    yn = yc * lax.rsqrt(var + GN_EPS) * gw[...] + gb[...]
    kv = k[...]
    k_a = ka[...]
    kd_sum = kv * (1.0 + (af[...] - 1.0) * k_a) + kv * (1.0 + (ab[...] - 1.0) * k_a)
    bonus = _group_sum(r[...] * kd_sum * rk[...], ones_bd) * v[...]
    o_ref[...] = ((yn + bonus) * g[...]).astype(o_ref.dtype)


def rwkv_post(yf, yb, r, k, v, a2, g, k_a, r_k, gn_w, gn_b, *, tm=512):
    t, d = r.shape
    ng = d // GROUP_LANES
    blk = pl.BlockSpec((tm, GROUP_LANES), lambda i, j: (i, j))
    blk_b = pl.BlockSpec((tm, GROUP_LANES), lambda i, j: (i, ng + j))
    par = pl.BlockSpec((1, GROUP_LANES), lambda i, j: (0, j))
    return pl.pallas_call(
        _rwkv_post_kernel,
        grid=(t // tm, ng),
        in_specs=[blk] * 5 + [blk, blk_b, blk] + [par] * 4,
        out_specs=blk,
        out_shape=jax.ShapeDtypeStruct((t, d), BF16),
        name="rwkv_post",
        compiler_params=_params("parallel", "parallel"),
    )(yf, yb, r, k, v, a2, a2, g, k_a, r_k, gn_w, gn_b)


def _attn_kernel(sink_ref, q_ref, kp_ref, kc_ref, kn_ref, vp_ref, vc_ref, vn_ref, o_ref, *, seqs, n_heads):
    i = pl.program_id(0)
    blk = ATT_BLOCK
    span = blk + 2 * WINDOW
    first = seqs.starts_seq(i * blk)
    last = seqs.ends_seq((i + 1) * blk)
    qi = lax.broadcasted_iota(jnp.int32, (blk, span), 0)
    ci = lax.broadcasted_iota(jnp.int32, (blk, span), 1)
    dist_i = jnp.abs(qi - (ci - WINDOW))
    valid = (dist_i <= WINDOW) & ((ci >= WINDOW) | jnp.logical_not(first)) \
        & ((ci < WINDOW + blk) | jnp.logical_not(last))
    dist = dist_i.astype(F32)
    n_kv = n_heads // ATT_GROUP
    kv_batch = min(ATT_KV_BATCH, n_kv)
    outs = []
    for kv0 in range(0, n_kv, kv_batch):
        heads = [(kvh, kvh * ATT_GROUP + g) for kvh in range(kv0, kv0 + kv_batch) for g in range(ATT_GROUP)]
        k_span, v_span = {}, {}
        for kvh in range(kv0, kv0 + kv_batch):
            ks = slice(kvh * HEAD, (kvh + 1) * HEAD)
            k_span[kvh] = jnp.concatenate([kp_ref[:, ks], kc_ref[:, ks], kn_ref[:, ks]], axis=0)
            v_span[kvh] = jnp.concatenate([vp_ref[:, ks], vc_ref[:, ks], vn_ref[:, ks]], axis=0)
        s = [_dot_nt(q_ref[:, h * HEAD:(h + 1) * HEAD], k_span[kvh]) for kvh, h in heads]
        s = [jnp.where(valid, sc * (HEAD ** -0.5) - (2.0 ** (-8.0 * (h + 1) / n_heads)) * dist, NEG_INF)
             for sc, (_, h) in zip(s, heads)]
        m = [jnp.maximum(jnp.max(sc, axis=-1, keepdims=True), sink_ref[h]) for sc, (_, h) in zip(s, heads)]
        e = [jnp.exp(sc - mx) for sc, mx in zip(s, m)]
        denom = [jnp.sum(ex, axis=-1, keepdims=True) + jnp.exp(sink_ref[h] - mx)
                 for ex, mx, (_, h) in zip(e, m, heads)]
        pv = [_dot(ex.astype(BF16), v_span[kvh]) for ex, (kvh, _) in zip(e, heads)]
        outs += [o / dn for o, dn in zip(pv, denom)]
    o_ref[...] = jnp.concatenate(outs, axis=-1).astype(o_ref.dtype)


def window_attention(qkv, sink, seqs, *, d):
    t = qkv.shape[0]
    n_heads = d // HEAD
    kvw = (n_heads // ATT_GROUP) * HEAD
    nb = t // ATT_BLOCK
    kcol = d // kvw
    prev = lambda i: jnp.maximum(i - 1, 0)
    nxt = lambda i: jnp.minimum(i + 1, nb - 1)
    kv_specs = [pl.BlockSpec((ATT_BLOCK, kvw), lambda i, c=c, f=f: (f(i), c))
                for c in (kcol, kcol + 1) for f in (prev, lambda i: i, nxt)]
    return pl.pallas_call(
        functools.partial(_attn_kernel, seqs=seqs, n_heads=n_heads),
        grid=(nb,),
        in_specs=[pl.BlockSpec(memory_space=pltpu.SMEM),
                  pl.BlockSpec((ATT_BLOCK, d), lambda i: (i, 0))] + kv_specs,
        out_specs=pl.BlockSpec((ATT_BLOCK, d), lambda i: (i, 0)),
        out_shape=jax.ShapeDtypeStruct((t, d), BF16),
        name="window_attention",
        compiler_params=_params("parallel"),
    )(sink, qkv, qkv, qkv, qkv, qkv, qkv, qkv)


def _pack_rows(ref, y):
    m = y.shape[0]
    half = y.shape[1] // 2
    ns = half // TILE_LANES
    lo = pltpu.bitcast(y[:, :half].astype(BF16).astype(F32), jnp.uint32) >> 16
    hi = pltpu.bitcast(y[:, half:].astype(BF16).astype(F32), jnp.uint32) & jnp.uint32(0xFFFF0000)
    w = lo | hi
    for s in range(ns):
        ref[pl.ds(s, m, stride=ns), :] = w[:, s * TILE_LANES:(s + 1) * TILE_LANES]


def _unpack_rows(ref, s, m, ns):
    w = ref[pl.ds(s, m, stride=ns), :]
    lo = pltpu.bitcast(w << 16, F32)
    hi = pltpu.bitcast(w & jnp.uint32(0xFFFF0000), F32)
    return lo, hi


def _router_kernel(x_ref, g_ref, mod_ref, rw_ref, rb_ref, hp_ref, idx_ref, gate_ref, pos_ref, cnt_ref,
                   run_ref, *, tm):
    i = pl.program_id(0)

    @pl.when(i == 0)
    def _():
        run_ref[...] = jnp.zeros_like(run_ref)

    h = _modnorm(x_ref[...], g_ref[...], mod_ref[3:4, :], mod_ref[4:5, :])
    _pack_rows(hp_ref, h)
    logits = jnp.dot(h, rw_ref[...], preferred_element_type=F32, precision=lax.Precision.HIGHEST)
    scores = jax.nn.sigmoid(logits)
    n_e = scores.shape[1]
    lane = lax.broadcasted_iota(jnp.int32, scores.shape, 1).astype(F32)
    work = scores + rb_ref[...]
    sel = jnp.zeros_like(scores)
    hot, idx, gates = [], [], []
    for _ in range(TOP_K):
        m = jnp.max(work, axis=-1, keepdims=True)
        first = jnp.min(jnp.where(work == m, lane, float(n_e)), axis=-1, keepdims=True)
        one = lane == first
        work = jnp.where(one, -jnp.inf, work)
        hot.append(one)
        idx.append(first)
        gates.append(jnp.sum(jnp.where(one, scores, 0.0), axis=-1, keepdims=True))
        sel = sel + one.astype(F32)
    total = gates[0]
    for gj in gates[1:]:
        total = total + gj
    t_i = lax.broadcasted_iota(jnp.int32, (tm, tm), 0)
    s_i = lax.broadcasted_iota(jnp.int32, (tm, tm), 1)
    before = _dot((s_i < t_i).astype(BF16), sel.astype(BF16)) + run_ref[...]
    run_ref[...] = run_ref[...] + jnp.sum(sel, axis=0, keepdims=True)
    cnt_ref[...] = run_ref[...]
    slot = lax.broadcasted_iota(jnp.int32, (tm, SLOT), 1)
    idx_o = jnp.zeros((tm, SLOT), jnp.int32)
    gate_o = jnp.zeros((tm, SLOT), F32)
    pos_o = jnp.zeros((tm, SLOT), jnp.int32)
    for j in range(TOP_K):
        pos_j = jnp.sum(jnp.where(hot[j], before, 0.0), axis=-1, keepdims=True).astype(jnp.int32)
        idx_o = jnp.where(slot == j, idx[j].astype(jnp.int32), idx_o)
        gate_o = jnp.where(slot == j, gates[j] / total * ROUTED_SCALE, gate_o)
        pos_o = jnp.where(slot == j, pos_j, pos_o)
    idx_ref[...] = idx_o
    gate_ref[...] = gate_o
    pos_ref[...] = pos_o


def router(x, g, mod3, router_w, router_bias, seqs, *, tm=256):
    t, d = x.shape
    n_e = router_w.shape[1]
    ns = d // (2 * TILE_LANES)
    tok = lambda i: (i, 0)
    return pl.pallas_call(
        functools.partial(_router_kernel, tm=tm),
        grid=(t // tm,),
        in_specs=[pl.BlockSpec((tm, d), tok),
                  pl.BlockSpec((1, d), lambda i: (0, 0)),
                  pl.BlockSpec((None, 6, d), lambda i: (seqs.seq_of_tile(i, tm), 0, 0)),
                  pl.BlockSpec((d, n_e), lambda i: (0, 0)),
                  pl.BlockSpec((1, n_e), lambda i: (0, 0))],
        out_specs=[pl.BlockSpec((tm * ns, TILE_LANES), tok),
                   pl.BlockSpec((tm, SLOT), tok), pl.BlockSpec((tm, SLOT), tok), pl.BlockSpec((tm, SLOT), tok),
                   pl.BlockSpec((1, n_e), lambda i: (0, 0))],
        out_shape=[jax.ShapeDtypeStruct((t * ns, TILE_LANES), jnp.uint32),
                   jax.ShapeDtypeStruct((t, SLOT), jnp.int32),
                   jax.ShapeDtypeStruct((t, SLOT), F32),
                   jax.ShapeDtypeStruct((t, SLOT), jnp.int32),
                   jax.ShapeDtypeStruct((1, n_e), F32)],
        scratch_shapes=[pltpu.VMEM((1, n_e), F32)],
        name="moe_router",
        compiler_params=_params("arbitrary"),
    )(x, g, mod3, router_w, router_bias)


def _ffn_rows(xs_ref, wg_ref, wu_ref, wd_ref, xb_ref):
    bm = xb_ref.shape[0]
    half = xb_ref.shape[1] // 2
    ns = half // TILE_LANES
    for s in range(ns):
        lo, hi = _unpack_rows(xs_ref, s, bm, ns)
        xb_ref[:, s * TILE_LANES:(s + 1) * TILE_LANES] = lo.astype(BF16)
        xb_ref[:, half + s * TILE_LANES:half + (s + 1) * TILE_LANES] = hi.astype(BF16)
    xb = xb_ref[...]
    hg = _dot(xb, wg_ref[...])
    hu = _dot(xb, wu_ref[...])
    hb = (hg * jax.nn.sigmoid(hg) * hu).astype(BF16)
    return _dot(hb, wd_ref[...])


def _shared_dispatch_kernel(dest_ref, hp_ref, wg_ref, wu_ref, wd_ref, zeros_ref, o_ref, xs_ref, xb_ref, sem,
                            *, tm, ns):
    del zeros_ref

    def row_copy(t, j):
        src = hp_ref.at[pl.ds(pl.multiple_of(t * ns, ns), ns)]
        dst = xs_ref.at[pl.ds(pl.multiple_of(dest_ref[t * SLOT + j], ns), ns)]
        return pltpu.make_async_copy(src, dst, sem)

    def issue(t, carry):
        for j in range(TOP_K):
            row_copy(t, j).start(priority=j % 2)
        return carry

    def drain(t, carry):
        for j in range(TOP_K):
            row_copy(t, j).wait()
        return carry

    lax.fori_loop(0, tm, issue, 0, unroll=DMA_LOOP_UNROLL)
    o_ref[...] = _ffn_rows(hp_ref, wg_ref, wu_ref, wd_ref, xb_ref)
    lax.fori_loop(0, tm, drain, 0, unroll=DMA_LOOP_UNROLL)


def shared_ffn_dispatch(hp, dest_rows, n_rows, w_gate, w_up, w_down, *, tm=256):
    d, f = w_gate.shape
    ns = d // (2 * TILE_LANES)
    t = hp.shape[0] // ns
    zeros = jnp.zeros((n_rows * ns, TILE_LANES), jnp.uint32)
    const = lambda i: (0, 0)
    return pl.pallas_call(
        functools.partial(_shared_dispatch_kernel, tm=tm, ns=ns),
        grid=(t // tm,),
        in_specs=[pl.BlockSpec((tm * SLOT,), lambda i: (i,), memory_space=pltpu.SMEM),
                  pl.BlockSpec((tm * ns, TILE_LANES), lambda i: (i, 0)),
                  pl.BlockSpec((d, f), const), pl.BlockSpec((d, f), const), pl.BlockSpec((f, d), const),
                  pl.BlockSpec(memory_space=pl.ANY)],
        out_specs=[pl.BlockSpec((tm, d), lambda i: (i, 0)), pl.BlockSpec(memory_space=pl.ANY)],
        out_shape=[jax.ShapeDtypeStruct((t, d), F32),
                   jax.ShapeDtypeStruct((n_rows * ns, TILE_LANES), jnp.uint32)],
        scratch_shapes=[pltpu.VMEM((tm, d), BF16), pltpu.SemaphoreType.DMA(())],
        input_output_aliases={5: 1},
        name="moe_shared_expert_dispatch",
        compiler_params=_params("arbitrary"),
    )(dest_rows, hp, w_gate, w_up, w_down, zeros)


def _expert_kernel(be_ref, nblk_ref, xs_ref, wg_ref, wu_ref, wd_ref, o_ref, xb_ref):
    i = pl.program_id(0)

    @pl.when(i < nblk_ref[0])
    def _():
        _pack_rows(o_ref, _ffn_rows(xs_ref, wg_ref, wu_ref, wd_ref, xb_ref))

    @pl.when(i >= nblk_ref[0])
    def _():
        o_ref[...] = jnp.zeros_like(o_ref)


def expert_ffn(xs, blk_expert, n_used, w_gate, w_up, w_down, *, bm):
    d, f = w_gate.shape[1:]
    ns = d // (2 * TILE_LANES)
    n_rows = xs.shape[0] // ns
    rows = pl.BlockSpec((bm * ns, TILE_LANES), lambda i, be, nb: (i, 0))
    return pl.pallas_call(
        _expert_kernel,
        grid_spec=pltpu.PrefetchScalarGridSpec(
            num_scalar_prefetch=2,
            grid=(n_rows // bm,),
            in_specs=[rows,
                      pl.BlockSpec((None, d, f), lambda i, be, nb: (be[i], 0, 0)),
                      pl.BlockSpec((None, d, f), lambda i, be, nb: (be[i], 0, 0)),
                      pl.BlockSpec((None, f, d), lambda i, be, nb: (be[i], 0, 0))],
            out_specs=rows,
            scratch_shapes=[pltpu.VMEM((bm, d), BF16)]),
        out_shape=jax.ShapeDtypeStruct((n_rows * ns, TILE_LANES), jnp.uint32),
        name="moe_experts",
        compiler_params=_params("arbitrary"),
    )(blk_expert, n_used, xs, w_gate, w_up, w_down)


def _combine_kernel(dest_ref, dest_next_ref, ys_ref, gate_ref, sh_ref, x_ref, mod_ref, o_ref, buf_ref, sems,
                    *, tm, ns):
    i = pl.program_id(0)
    n = pl.num_programs(0)
    slot = lax.rem(i, 2)

    def row_copy(dref, t, j, sl):
        src = ys_ref.at[pl.ds(pl.multiple_of(dref[t * SLOT + j], ns), ns)]
        dst = buf_ref.at[sl * TOP_K + j, pl.ds(pl.multiple_of(t * ns, ns), ns)]
        return pltpu.make_async_copy(src, dst, sems.at[sl])

    def gather(dref, sl):
        def issue(t, carry):
            for j in range(TOP_K):
                row_copy(dref, t, j, sl).start()
            return carry
        lax.fori_loop(0, tm, issue, 0, unroll=DMA_LOOP_UNROLL)

    @pl.when(i == 0)
    def _():
        gather(dest_ref, 0)

    @pl.when(i + 1 < n)
    def _():
        gather(dest_next_ref, 1 - slot)

    def drain(t, carry):
        for j in range(TOP_K):
            row_copy(dest_ref, t, j, slot).wait()
        return carry

    lax.fori_loop(0, tm, drain, 0, unroll=DMA_LOOP_UNROLL)
    gates = gate_ref[...]
    gj = [jnp.broadcast_to(gates[:, j:j + 1], (tm, TILE_LANES)) for j in range(TOP_K)]
    gate_f = mod_ref[5:6, :]
    half = x_ref.shape[1] // 2
    for s in range(ns):
        acc_lo = jnp.zeros((tm, TILE_LANES), F32)
        acc_hi = jnp.zeros((tm, TILE_LANES), F32)
        for j in range(TOP_K):
            lo, hi = _unpack_rows(buf_ref.at[slot * TOP_K + j], s, tm, ns)
            acc_lo = acc_lo + gj[j] * lo
            acc_hi = acc_hi + gj[j] * hi
        for acc, off in ((acc_lo, s * TILE_LANES), (acc_hi, half + s * TILE_LANES)):
            sl = slice(off, off + TILE_LANES)
            o_ref[:, sl] = x_ref[:, sl] + gate_f[:, sl] * (acc + sh_ref[:, sl])


def combine(ys, dest_rows, gates, shared, x, mod3, seqs, *, tm=128):
    t, d = x.shape
    ns = d // (2 * TILE_LANES)
    n = t // tm
    tok = lambda i: (i, 0)
    return pl.pallas_call(
        functools.partial(_combine_kernel, tm=tm, ns=ns),
        grid=(n,),
        in_specs=[pl.BlockSpec((tm * SLOT,), lambda i: (i,), memory_space=pltpu.SMEM),
                  pl.BlockSpec((tm * SLOT,), lambda i: (jnp.minimum(i + 1, n - 1),), memory_space=pltpu.SMEM),
                  pl.BlockSpec(memory_space=pl.ANY),
                  pl.BlockSpec((tm, SLOT), tok),
                  pl.BlockSpec((tm, d), tok),
                  pl.BlockSpec((tm, d), tok),
                  pl.BlockSpec((None, 6, d), lambda i: (seqs.seq_of_tile(i, tm), 0, 0))],
        out_specs=pl.BlockSpec((tm, d), tok),
        out_shape=jax.ShapeDtypeStruct((t, d), F32),
        scratch_shapes=[pltpu.VMEM((2 * TOP_K, tm * ns, TILE_LANES), jnp.uint32),
                        pltpu.SemaphoreType.DMA((2,))],
        name="moe_combine",
        compiler_params=_params("arbitrary"),
    )(dest_rows, dest_rows, ys, gates, shared, x, mod3)


def moe_block(x, mod3, g, router_w, router_bias, w_gate, w_up, w_down, sh_gate, sh_up, sh_down, layer, seqs,
              *, bm=256):
    t, d = x.shape
    n_e = router_w.shape[1]
    hp, idx, gates, pos, counts = router(x, g, mod3, router_w, router_bias, seqs)
    cnt = counts[0].astype(jnp.int32)
    padded = (cnt + bm - 1) // bm * bm
    end_pad = jnp.cumsum(padded)
    start_pad = end_pad - padded
    ns = d // (2 * TILE_LANES)
    dest_rows = ((jnp.take(start_pad, idx, axis=0) + pos) * ns).reshape(-1)
    nblk = -(-(t * TOP_K) // bm) + n_e
    blk_start = jnp.arange(nblk, dtype=jnp.int32) * bm
    blk_expert = jnp.minimum(jnp.sum(end_pad[None, :] <= blk_start[:, None], axis=1), n_e - 1).astype(jnp.int32)
    n_used = (end_pad[-1:] // bm).astype(jnp.int32)
    sh = lambda w: cast_bf16(w[:, None], layer)[0]
    shared, xs = shared_ffn_dispatch(hp, dest_rows, nblk * bm, sh(sh_gate), sh(sh_up), sh(sh_down))
    bf = lambda w: cast_bf16(w, layer)
    ys = expert_ffn(xs, blk_expert, n_used, bf(w_gate), bf(w_up), bf(w_down), bm=bm)
    return combine(ys, dest_rows, gates, shared, x, mod3, seqs)


def _final_norm_kernel(x_ref, g_ref, o_ref):
    x = x_ref[...]
    o_ref[...] = x * lax.rsqrt(jnp.mean(x * x, axis=-1, keepdims=True) + RMS_EPS) * g_ref[...]


def final_rmsnorm(x, g, row0, rows, *, tm=256):
    d = x.shape[1]
    b0 = row0 // tm
    return pl.pallas_call(
        _final_norm_kernel,
        grid=(rows // tm,),
        in_specs=[pl.BlockSpec((tm, d), lambda i: (b0 + i, 0)), pl.BlockSpec((1, d), lambda i: (0, 0))],
        out_specs=pl.BlockSpec((tm, d), lambda i: (i, 0)),
        out_shape=jax.ShapeDtypeStruct((rows, d), F32),
        name="final_norm",
        compiler_params=_params("parallel"),
    )(x, g)


def _block_diag2(w):
    k, n = w.shape[1:]
    z = jnp.zeros((k, n), w.dtype)
    return jnp.concatenate([jnp.concatenate([w[0], z], axis=1), jnp.concatenate([z, w[1]], axis=1)], axis=0)


def rwkv_block(x, mod3, g, seqs, mu, w_r, w_k, w_v, w_o, dec_w0, dec_w1, dec_w2, iclr_a0, iclr_a1, iclr_a2,
               gate_g1, gate_g2, k_k, k_a, r_k, gn_w, gn_b):
    d = x.shape[1]
    bf = lambda w: w.astype(BF16)
    xr, xw, xk, xv, xa, xg = shiftmix(x, g, mod3, mu, seqs)
    r = mm(xr, bf(w_r), out_dtype=F32, name="rwkv_proj")
    k = mm(xk, bf(w_k), out_dtype=F32, name="rwkv_proj")
    v = mm(xv, bf(w_v), out_dtype=F32, name="rwkv_proj")
    lw_in = mm(xw, bf(jnp.concatenate([dec_w1[0], dec_w1[1]], axis=1)), out_dtype=BF16, epi=_epi_tanh,
               name="rwkv_decay_lora_in")
    la_in = mm(xa, bf(jnp.concatenate([iclr_a1[0], iclr_a1[1]], axis=1)), out_dtype=BF16, name="rwkv_iclr_lora_in")
    lg_in = mm(xg, bf(gate_g1), out_dtype=BF16, epi=_epi_sigmoid, name="rwkv_gate_lora_in")
    gate = mm(lg_in, bf(gate_g2), out_dtype=F32, name="rwkv_gate_lora_out")
    lw2 = mm(lw_in, bf(_block_diag2(dec_w2)), out_dtype=F32, epi=_epi_log_decay,
             extras=[(dec_w0.reshape(1, 2 * d), "col")], name="rwkv_log_decay")
    a2 = mm(la_in, bf(_block_diag2(iclr_a2)), out_dtype=F32, epi=_epi_bias_sigmoid,
            extras=[(iclr_a0.reshape(1, 2 * d), "col")], name="rwkv_iclr")
    k_a2 = k_a.reshape(1, d)
    yf, yb = wkv(r, k, v, lw2, a2, k_k.reshape(1, d), k_a2, seqs)
    yo = rwkv_post(yf, yb, r, k, v, a2, gate, k_a2, r_k.reshape(1, d), gn_w.reshape(1, d), gn_b.reshape(1, d))
    return mm(yo, bf(w_o), out_dtype=F32, epi=_epi_residual(2), extras=[(x, "row"), (mod3, "seq")], seqs=seqs,
              name="mixer_out_proj")


def attention_block(x, mod3, g, seqs, w_qkv, w_o, sink):
    d = x.shape[1]
    h = modnorm(x, g, mod3, seqs, row=0)
    qkv = mm(h, w_qkv.astype(BF16), out_dtype=BF16, name="attn_qkv")
    o = window_attention(qkv, sink, seqs, d=d)
    return mm(o, w_o.astype(BF16), out_dtype=F32, epi=_epi_residual(2), extras=[(x, "row"), (mod3, "seq")],
              seqs=seqs, name="mixer_out_proj")


def kernel(x_prompt, x_sample, c_prompt, c_sample, ada_w, ada_b, norm_mix, norm_ffn, rw_mu, rw_wr, rw_wk, rw_wv, rw_wo, rw_dec_w0, rw_dec_w1, rw_dec_w2, rw_iclr_a0, rw_iclr_a1, rw_iclr_a2, rw_gate_g1, rw_gate_g2, rw_k_k, rw_k_a, rw_r_k, rw_gn_w, rw_gn_b, at_w_qkv, at_w_o, at_sink, router_w, router_bias, moe_w_gate, moe_w_up, moe_w_down, sh_w_gate, sh_w_up, sh_w_down, final_norm):
    b1, s1, d = x_prompt.shape
    b2, s2, _ = x_sample.shape
    seqs = Seqs(b1, s1, b2, s2)
    depth = ada_w.shape[0]
    x = jnp.concatenate([x_prompt.reshape(b1 * s1, d), x_sample.reshape(b2 * s2, d)], axis=0)
    c = jnp.concatenate([c_prompt, c_sample], axis=0)
    rows = -(-seqs.nseq // TILE_SUBLANES) * TILE_SUBLANES
    c_act = jnp.pad(jax.nn.silu(c), ((0, rows - seqs.nseq), (0, 0)))
    for i in range(depth):
        j = i // 2
        mod = mm(c_act, ada_w[i], out_dtype=F32, epi=_epi_bias, extras=[(ada_b[i].reshape(1, -1), "col")],
                 nb=d, precision=lax.Precision.HIGHEST, name="adaln_mod")
        mod3 = mod[:seqs.nseq].reshape(seqs.nseq, 6, d)
        g_mix = norm_mix[i].reshape(1, d)
        if i % 2 == 0:
            x = rwkv_block(x, mod3, g_mix, seqs, rw_mu[j], rw_wr[j], rw_wk[j], rw_wv[j], rw_wo[j], rw_dec_w0[j],
                           rw_dec_w1[j], rw_dec_w2[j], rw_iclr_a0[j], rw_iclr_a1[j], rw_iclr_a2[j],
                           rw_gate_g1[j], rw_gate_g2[j], rw_k_k[j], rw_k_a[j], rw_r_k[j], rw_gn_w[j], rw_gn_b[j])
        else:
            x = attention_block(x, mod3, g_mix, seqs, at_w_qkv[j], at_w_o[j], at_sink[j])
        x = moe_block(x, mod3, norm_ffn[i].reshape(1, d), router_w[i], router_bias[i].reshape(1, -1),
                      moe_w_gate, moe_w_up, moe_w_down, sh_w_gate, sh_w_up, sh_w_down, i, seqs)
    fw = final_norm.reshape(1, d)
    y1 = final_rmsnorm(x, fw, 0, seqs.t1).reshape(b1, s1, d)
    y2 = final_rmsnorm(x, fw, seqs.t1, b2 * s2).reshape(b2, s2, d)
    return (y1, y2)
```

```python
import functools
from typing import NamedTuple

import jax
import jax.numpy as jnp
from jax import lax
from jax.experimental import pallas as pl
from jax.experimental.pallas import tpu as pltpu

F32 = jnp.float32
BF16 = jnp.bfloat16

HEAD = 64
GROUP_LANES = 256
CHUNK = 64
ATT_GROUP = 4
ATT_KV_BATCH = 1
WINDOW = 128
ATT_BLOCK = 128
TOP_K = 6
ROUTED_SCALE = 2.446
RMS_EPS = 1e-6
GN_EPS = 64e-5
NEG_INF = -1e30
SLOT = 8
DMA_LOOP_UNROLL = 8
TILE_SUBLANES = 8
TILE_LANES = 128
VMEM_LIMIT = 56 * 1024 * 1024


class Seqs(NamedTuple):
    b1: int
    s1: int
    b2: int
    s2: int

    @property
    def t1(self):
        return self.b1 * self.s1

    @property
    def total(self):
        return self.b1 * self.s1 + self.b2 * self.s2

    @property
    def nseq(self):
        return self.b1 + self.b2

    def seq_of_tile(self, i, tm):
        n1 = self.t1 // tm
        return jnp.where(i < n1, i // (self.s1 // tm), self.b1 + (i - n1) // (self.s2 // tm))

    def starts_seq(self, tok):
        return jnp.where(tok < self.t1, lax.rem(tok, self.s1) == 0, lax.rem(tok - self.t1, self.s2) == 0)

    def ends_seq(self, tok_end):
        return jnp.where(tok_end <= self.t1, lax.rem(tok_end, self.s1) == 0,
                         lax.rem(tok_end - self.t1, self.s2) == 0)


def _params(*sem):
    return pltpu.CompilerParams(dimension_semantics=sem, vmem_limit_bytes=VMEM_LIMIT)


def _modnorm(x, g, shift, scale):
    y = x * lax.rsqrt(jnp.mean(x * x, axis=-1, keepdims=True) + RMS_EPS)
    return (y * g) * (1.0 + scale) + shift


def _split_bf16(x):
    hi = x.astype(BF16)
    lo = (x - hi.astype(F32)).astype(BF16)
    return hi, lo


def _dot(a, b):
    return jnp.dot(a, b, preferred_element_type=F32)


def _dot_nt(a, b):
    return lax.dot_general(a, b, (((1,), (1,)), ((), ())), preferred_element_type=F32)


def _dot_tn(a, b):
    return lax.dot_general(a, b, (((0,), (0,)), ((), ())), preferred_element_type=F32)


def _block_mask():
    r = lax.broadcasted_iota(jnp.int32, (GROUP_LANES, GROUP_LANES), 0) // HEAD
    c = lax.broadcasted_iota(jnp.int32, (GROUP_LANES, GROUP_LANES), 1) // HEAD
    return r == c


def _group_sum(x, ones_bd):
    hi, lo = _split_bf16(x)
    return _dot(hi, ones_bd) + _dot(lo, ones_bd)


def _modnorm_kernel(x_ref, g_ref, mod_ref, o_ref, *, row):
    h = _modnorm(x_ref[...], g_ref[...], mod_ref[row:row + 1, :], mod_ref[row + 1:row + 2, :])
    o_ref[...] = h.astype(o_ref.dtype)


def modnorm(x, g, mod3, seqs, *, row, tm=256):
    t, d = x.shape
    return pl.pallas_call(
        functools.partial(_modnorm_kernel, row=row),
        grid=(t // tm,),
        in_specs=[pl.BlockSpec((tm, d), lambda i: (i, 0)),
                  pl.BlockSpec((1, d), lambda i: (0, 0)),
                  pl.BlockSpec((None, 6, d), lambda i: (seqs.seq_of_tile(i, tm), 0, 0))],
        out_specs=pl.BlockSpec((tm, d), lambda i: (i, 0)),
        out_shape=jax.ShapeDtypeStruct((t, d), BF16),
        name="modnorm",
        compiler_params=_params("parallel"),
    )(x, g, mod3)


def _shiftmix_kernel(x_ref, xp_ref, xn_ref, g_ref, mod_ref, mu_ref, *o_refs, seqs, tm):
    i = pl.program_id(0)
    g = g_ref[...]
    shift = mod_ref[0:1, :]
    scale = mod_ref[1:2, :]
    h = _modnorm(x_ref[...], g, shift, scale)
    hp = _modnorm(xp_ref[...], g, shift, scale)[TILE_SUBLANES - 1:TILE_SUBLANES, :]
    hn = _modnorm(xn_ref[...], g, shift, scale)[0:1, :]
    hp = jnp.where(seqs.starts_seq(i * tm), 0.0, hp)
    hn = jnp.where(seqs.ends_seq((i + 1) * tm), 0.0, hn)
    rows = lax.broadcasted_iota(jnp.int32, h.shape, 0)
    prev = jnp.where(rows == 0, hp, pltpu.roll(h, 1, 0))
    nxt = jnp.where(rows == tm - 1, hn, pltpu.roll(h, tm - 1, 0))
    xx = 0.5 * (prev + nxt) - h
    for j, o_ref in enumerate(o_refs):
        o_ref[...] = (h + xx * mu_ref[j:j + 1, :]).astype(o_ref.dtype)


def shiftmix(x, g, mod3, mu, seqs, *, tm=256):
    t, d = x.shape
    nb8 = t // TILE_SUBLANES
    r8 = tm // TILE_SUBLANES
    return pl.pallas_call(
        functools.partial(_shiftmix_kernel, seqs=seqs, tm=tm),
        grid=(t // tm,),
        in_specs=[pl.BlockSpec((tm, d), lambda i: (i, 0)),
                  pl.BlockSpec((TILE_SUBLANES, d), lambda i: (jnp.maximum(i * r8 - 1, 0), 0)),
                  pl.BlockSpec((TILE_SUBLANES, d), lambda i: (jnp.minimum((i + 1) * r8, nb8 - 1), 0)),
                  pl.BlockSpec((1, d), lambda i: (0, 0)),
                  pl.BlockSpec((None, 6, d), lambda i: (seqs.seq_of_tile(i, tm), 0, 0)),
                  pl.BlockSpec((6, d), lambda i: (0, 0))],
        out_specs=[pl.BlockSpec((tm, d), lambda i: (i, 0))] * 6,
        out_shape=[jax.ShapeDtypeStruct((t, d), BF16)] * 6,
        name="rwkv_shiftmix",
        compiler_params=_params("parallel"),
    )(x, x, x, g, mod3, mu)


def _cast_kernel(w_ref, o_ref):
    o_ref[...] = w_ref[...].astype(o_ref.dtype)


def cast_bf16(w, layer):
    _, e, k, n = w.shape
    return pl.pallas_call(
        _cast_kernel,
        grid=(e,),
        in_specs=[pl.BlockSpec((None, None, k, n), lambda i: (layer, i, 0, 0))],
        out_specs=pl.BlockSpec((None, k, n), lambda i: (i, 0, 0)),
        out_shape=jax.ShapeDtypeStruct((e, k, n), BF16),
        name="cast_bf16",
        compiler_params=_params("parallel"),
    )(w)


def _mm_kernel(a_ref, w_ref, *refs, epi, tn, n_extra, precision):
    extra = refs[:n_extra]
    o_ref = refs[n_extra]
    n = o_ref.shape[-1]
    a = a_ref[...]
    for c in range(n // tn):
        sl = slice(c * tn, (c + 1) * tn)
        acc = jnp.dot(a, w_ref[:, sl], preferred_element_type=F32, precision=precision)
        if epi is not None:
            acc = epi(acc, *[e[:, sl] for e in extra])
        o_ref[:, sl] = acc.astype(o_ref.dtype)


def mm(a, w, *, out_dtype, name, epi=None, extras=(), seqs=None, tm=512, tn=512, nb=None, precision=None):
    m, k = a.shape
    n = w.shape[1]
    nb = n if nb is None else nb
    tn = max([c for c in range(TILE_LANES, min(tn, nb) + 1, TILE_LANES) if nb % c == 0], default=nb)
    tm = min(tm, m)
    specs = [pl.BlockSpec((tm, k), lambda i, j: (i, 0)), pl.BlockSpec((k, nb), lambda i, j: (0, j))]
    args = [a, w]
    for arr, kind in extras:
        if kind == "col":
            specs.append(pl.BlockSpec((1, nb), lambda i, j: (0, j)))
        elif kind == "row":
            specs.append(pl.BlockSpec((tm, nb), lambda i, j: (i, j)))
        else:
            specs.append(pl.BlockSpec((None, arr.shape[1], nb),
                                      lambda i, j: (seqs.seq_of_tile(i, tm), 0, j)))
        args.append(arr)
    return pl.pallas_call(
        functools.partial(_mm_kernel, epi=epi, tn=tn, n_extra=len(extras), precision=precision),
        grid=(m // tm, n // nb),
        in_specs=specs,
        out_specs=pl.BlockSpec((tm, nb), lambda i, j: (i, j)),
        out_shape=jax.ShapeDtypeStruct((m, n), out_dtype),
        name=name,
        compiler_params=_params("parallel", "arbitrary"),
    )(*args)


def _epi_bias(acc, b):
    return acc + b


def _epi_tanh(acc):
    return jnp.tanh(acc)


def _epi_sigmoid(acc):
    return jax.nn.sigmoid(acc)


def _epi_log_decay(acc, w0):
    y = -(acc + w0)
    softplus = jnp.maximum(y, 0.0) + jnp.log1p(jnp.exp(-jnp.abs(y)))
    return -jnp.exp(-softplus - 0.5)


def _epi_bias_sigmoid(acc, b):
    return jax.nn.sigmoid(acc + b)


def _epi_residual(row):
    def epi(acc, x, mod):
        return x + mod[row:row + 1, :] * acc
    return epi


def _expand_bd(m, bm):
    mb = m.astype(BF16)
    return jnp.where(bm, jnp.concatenate([mb] * (GROUP_LANES // HEAD), axis=0), jnp.zeros((), BF16))


def _wkv_chunks(chains, bm, ones_bd):
    c = CHUNK
    n = len(chains)
    rev = [ch[8] for ch in chains]
    t_i = lax.broadcasted_iota(jnp.int32, (c, c), 0)
    s_i = lax.broadcasted_iota(jnp.int32, (c, c), 1)
    tri = {False: (s_i <= t_i).astype(BF16), True: (s_i >= t_i).astype(BF16)}
    t2 = lax.broadcasted_iota(jnp.int32, (c, GROUP_LANES), 0)
    s2 = lax.broadcasted_iota(jnp.int32, (c, GROUP_LANES), 1) % c
    strict = {False: s2 < t2, True: s2 > t2}
    incl = {False: s2 <= t2, True: s2 >= t2}
    eye = jnp.where(s2 == t2, 1.0, 0.0)

    k2 = [ch[1] * ch[5] for ch in chains]
    sq = [piece for q in k2 for piece in _split_bf16(q * q)]
    ss_all = _dot(jnp.concatenate(sq, axis=0), ones_bd)
    ss = [ss_all[2 * j * c:(2 * j + 1) * c] + ss_all[(2 * j + 1) * c:(2 * j + 2) * c] for j in range(n)]
    kk = [q / jnp.maximum(jnp.sqrt(s), 1e-12) for q, s in zip(k2, ss)]
    kd = [ch[1] * (1.0 + (ch[4] - 1.0) * ch[6]) for ch in chains]
    cw = [None] * n
    for rv in (False, True):
        ids = [j for j in range(n) if rev[j] == rv]
        if ids:
            lw = jnp.concatenate([piece for j in ids for piece in _split_bf16(chains[j][3])], axis=1)
            cw_all = _dot(tri[rv], lw)
            for pos, j in enumerate(ids):
                base = 2 * pos * GROUP_LANES
                cw[j] = (cw_all[:, base:base + GROUP_LANES]
                         + cw_all[:, base + GROUP_LANES:base + 2 * GROUP_LANES])
    edge = [w[0:1, :] if rv else w[c - 1:c, :] for w, rv in zip(cw, rev)]
    e_neg = [jnp.exp(-w) for w in cw]
    al = [-q * jnp.exp(w - ch[3]) for q, w, ch in zip(kk, cw, chains)]
    rb = [ch[0] * jnp.exp(w) for w, ch in zip(cw, chains)]
    ka = [q * ch[4] for q, ch in zip(kk, chains)]
    be = [q * e for q, e in zip(ka, e_neg)]
    kb = [q * e for q, e in zip(kd, e_neg)]
    x = [jnp.concatenate([p, q], axis=0).astype(BF16) for p, q in zip(al, rb)]
    z_bd = [jnp.concatenate([_expand_bd(p, bm), _expand_bd(q, bm)], axis=0) for p, q in zip(be, kb)]
    a_all = [_dot_nt(p, q) for p, q in zip(x, z_bd)]
    a_ab = [jnp.where(strict[rv], m[:c, :GROUP_LANES], 0.0) for m, rv in zip(a_all, rev)]
    a_kr = [jnp.concatenate([jnp.where(strict[rv], m[:c, GROUP_LANES:], 0.0),
                             jnp.where(incl[rv], m[c:, GROUP_LANES:], 0.0)], axis=0).astype(BF16)
            for m, rv in zip(a_all, rev)]
    a_rb = [jnp.where(incl[rv], m[c:, :GROUP_LANES], 0.0).astype(BF16) for m, rv in zip(a_all, rev)]
    p = [eye + m for m in a_ab]
    ap = [_dot(m.astype(BF16), _expand_bd(m, bm)) for m in a_ab]
    for _ in range(4):
        both = [_dot(jnp.concatenate([m, q], axis=0).astype(BF16), _expand_bd(m, bm)) for m, q in zip(ap, p)]
        p = [q + b[c:] for q, b in zip(p, both)]
        ap = [b[:c] for b in both]
    p = [q + _dot(q.astype(BF16), _expand_bd(m, bm)) for q, m in zip(p, ap)]
    xh = [_dot_nt(q, ch[7].astype(BF16)) for q, ch in zip(x, chains)]
    av = [_dot(m, _expand_bd(ch[2], bm)) for m, ch in zip(a_kr, chains)]
    u = [_dot(q.astype(BF16), _expand_bd(h[:c] + w[:c], bm)) for q, h, w in zip(p, xh, av)]
    y = [h[c:] + w[c:] + _dot(m, _expand_bd(q, bm)) for h, w, m, q in zip(xh, av, a_rb, u)]
    e_rest = [jnp.exp(e - w) for e, w in zip(edge, cw)]
    zw = [jnp.concatenate([p_ * e, q * e], axis=0).astype(BF16) for p_, q, e in zip(ka, kd, e_rest)]
    uv = [jnp.concatenate([q, ch[2]], axis=0).astype(BF16) for q, ch in zip(u, chains)]
    upd = [_dot_tn(p_, q) for p_, q in zip(uv, zw)]
    ht_new = [chains[j][7] * jnp.exp(edge[j]) + jnp.where(bm, upd[j], 0.0) for j in range(n)]
    return y, ht_new


def _wkv_kernel(rf, kf, vf, lwf, af, rb, kb, vb, lwb, ab, kk_ref, ka_ref, yf_ref, yb_ref, st_ref,
                *, seqs, ct, ngb):
    i = pl.program_id(1)
    nt = pl.num_programs(1)
    nc = ct // CHUNK
    bm = _block_mask()
    ones_bd = bm.astype(BF16)

    def load(refs, lanes, off, st_idx, reset, reverse):
        rows = pl.ds(off, CHUNK)
        r, k, v, lw, a = [ref[rows, lanes] for ref in refs]
        ht = jnp.where(reset, 0.0, st_ref[st_idx])
        return (r, k, v, lw, a, kk_ref[:, lanes], ka_ref[:, lanes], ht, reverse)

    def body(c, carry):
        off_f = pl.multiple_of(c * CHUNK, CHUNK)
        off_b = pl.multiple_of((nc - 1 - c) * CHUNK, CHUNK)
        tok_b = (nt - 1 - i) * ct + off_b
        reset_f = seqs.starts_seq(i * ct + off_f)
        reset_b = seqs.ends_seq(tok_b + CHUNK)
        lanes = [slice(g * GROUP_LANES, (g + 1) * GROUP_LANES) for g in range(ngb)]
        chains = [load((rf, kf, vf, lwf, af), lanes[g], off_f, g, reset_f, False) for g in range(ngb)]
        chains += [load((rb, kb, vb, lwb, ab), lanes[g], off_b, ngb + g, reset_b, True) for g in range(ngb)]
        y, ht = _wkv_chunks(chains, bm, ones_bd)
        for g in range(ngb):
            st_ref[g] = ht[g]
            st_ref[ngb + g] = ht[ngb + g]
            yf_ref[pl.ds(off_f, CHUNK), lanes[g]] = y[g]
            yb_ref[pl.ds(off_b, CHUNK), lanes[g]] = y[ngb + g]
        return carry

    lax.fori_loop(0, nc, body, 0)


def wkv(r, k, v, lw2, a2, k_k, k_a, seqs, *, ct=256, ngb=4):
    t, d = r.shape
    ngb = min(ngb, d // GROUP_LANES)
    wl = ngb * GROUP_LANES
    nlb = d // wl
    nt = t // ct
    fwd = lambda g, i: (i, g)
    bwd = lambda g, i: (nt - 1 - i, g)
    bwd2 = lambda g, i: (nt - 1 - i, nlb + g)
    blk = (ct, wl)
    par = pl.BlockSpec((1, wl), lambda g, i: (0, g))
    return pl.pallas_call(
        functools.partial(_wkv_kernel, seqs=seqs, ct=ct, ngb=ngb),
        grid=(nlb, nt),
        in_specs=[pl.BlockSpec(blk, fwd)] * 5 + [pl.BlockSpec(blk, bwd)] * 3 + [pl.BlockSpec(blk, bwd2)] * 2
                 + [par, par],
        out_specs=[pl.BlockSpec(blk, fwd), pl.BlockSpec(blk, bwd)],
        out_shape=[jax.ShapeDtypeStruct((t, d), F32)] * 2,
        scratch_shapes=[pltpu.VMEM((2 * ngb, GROUP_LANES, GROUP_LANES), F32)],
        name="wkv_scan",
        compiler_params=_params("parallel", "arbitrary"),
    )(r, k, v, lw2, a2, r, k, v, lw2, a2, k_k, k_a)


def _rwkv_post_kernel(yf, yb, r, k, v, af, ab, g, ka, rk, gw, gb, o_ref):
    ones_bd = _block_mask().astype(BF16)
    inv_n = 1.0 / HEAD
    y = yf[...] + yb[...]
    mean = _group_sum(y, ones_bd) * inv_n
    yc = y - mean
    var = _group_sum(yc * yc, ones_bd) * inv_n
    yn = yc * lax.rsqrt(var + GN_EPS) * gw[...] + gb[...]
    kv = k[...]
    k_a = ka[...]
    kd_sum = kv * (1.0 + (af[...] - 1.0) * k_a) + kv * (1.0 + (ab[...] - 1.0) * k_a)
    bonus = _group_sum(r[...] * kd_sum * rk[...], ones_bd) * v[...]
    o_ref[...] = ((yn + bonus) * g[...]).astype(o_ref.dtype)


def rwkv_post(yf, yb, r, k, v, a2, g, k_a, r_k, gn_w, gn_b, *, tm=512):
    t, d = r.shape
    ng = d // GROUP_LANES
    blk = pl.BlockSpec((tm, GROUP_LANES), lambda i, j: (i, j))
    blk_b = pl.BlockSpec((tm, GROUP_LANES), lambda i, j: (i, ng + j))
    par = pl.BlockSpec((1, GROUP_LANES), lambda i, j: (0, j))
    return pl.pallas_call(
        _rwkv_post_kernel,
        grid=(t // tm, ng),
        in_specs=[blk] * 5 + [blk, blk_b, blk] + [par] * 4,
        out_specs=blk,
        out_shape=jax.ShapeDtypeStruct((t, d), BF16),
        name="rwkv_post",
        compiler_params=_params("parallel", "parallel"),
    )(yf, yb, r, k, v, a2, a2, g, k_a, r_k, gn_w, gn_b)


def _attn_kernel(sink_ref, q_ref, kp_ref, kc_ref, kn_ref, vp_ref, vc_ref, vn_ref, o_ref, *, seqs, n_heads):
    i = pl.program_id(0)
    blk = ATT_BLOCK
    span = blk + 2 * WINDOW
    first = seqs.starts_seq(i * blk)
    last = seqs.ends_seq((i + 1) * blk)
    qi = lax.broadcasted_iota(jnp.int32, (blk, span), 0)
    ci = lax.broadcasted_iota(jnp.int32, (blk, span), 1)
    dist_i = jnp.abs(qi - (ci - WINDOW))
    valid = (dist_i <= WINDOW) & ((ci >= WINDOW) | jnp.logical_not(first)) \
        & ((ci < WINDOW + blk) | jnp.logical_not(last))
    dist = dist_i.astype(F32)
    n_kv = n_heads // ATT_GROUP
    kv_batch = min(ATT_KV_BATCH, n_kv)
    outs = []
    for kv0 in range(0, n_kv, kv_batch):
        heads = [(kvh, kvh * ATT_GROUP + g) for kvh in range(kv0, kv0 + kv_batch) for g in range(ATT_GROUP)]
        k_span, v_span = {}, {}
        for kvh in range(kv0, kv0 + kv_batch):
            ks = slice(kvh * HEAD, (kvh + 1) * HEAD)
            k_span[kvh] = jnp.concatenate([kp_ref[:, ks], kc_ref[:, ks], kn_ref[:, ks]], axis=0)
            v_span[kvh] = jnp.concatenate([vp_ref[:, ks], vc_ref[:, ks], vn_ref[:, ks]], axis=0)
        s = [_dot_nt(q_ref[:, h * HEAD:(h + 1) * HEAD], k_span[kvh]) for kvh, h in heads]
        s = [jnp.where(valid, sc * (HEAD ** -0.5) - (2.0 ** (-8.0 * (h + 1) / n_heads)) * dist, NEG_INF)
             for sc, (_, h) in zip(s, heads)]
        m = [jnp.maximum(jnp.max(sc, axis=-1, keepdims=True), sink_ref[h]) for sc, (_, h) in zip(s, heads)]
        e = [jnp.exp(sc - mx) for sc, mx in zip(s, m)]
        denom = [jnp.sum(ex, axis=-1, keepdims=True) + jnp.exp(sink_ref[h] - mx)
                 for ex, mx, (_, h) in zip(e, m, heads)]
        pv = [_dot(ex.astype(BF16), v_span[kvh]) for ex, (kvh, _) in zip(e, heads)]
        outs += [o / dn for o, dn in zip(pv, denom)]
    o_ref[...] = jnp.concatenate(outs, axis=-1).astype(o_ref.dtype)


def window_attention(qkv, sink, seqs, *, d):
    t = qkv.shape[0]
    n_heads = d // HEAD
    kvw = (n_heads // ATT_GROUP) * HEAD
    nb = t // ATT_BLOCK
    kcol = d // kvw
    prev = lambda i: jnp.maximum(i - 1, 0)
    nxt = lambda i: jnp.minimum(i + 1, nb - 1)
    kv_specs = [pl.BlockSpec((ATT_BLOCK, kvw), lambda i, c=c, f=f: (f(i), c))
                for c in (kcol, kcol + 1) for f in (prev, lambda i: i, nxt)]
    return pl.pallas_call(
        functools.partial(_attn_kernel, seqs=seqs, n_heads=n_heads),
        grid=(nb,),
        in_specs=[pl.BlockSpec(memory_space=pltpu.SMEM),
                  pl.BlockSpec((ATT_BLOCK, d), lambda i: (i, 0))] + kv_specs,
        out_specs=pl.BlockSpec((ATT_BLOCK, d), lambda i: (i, 0)),
        out_shape=jax.ShapeDtypeStruct((t, d), BF16),
        name="window_attention",
        compiler_params=_params("parallel"),
    )(sink, qkv, qkv, qkv, qkv, qkv, qkv, qkv)


def _pack_rows(ref, y):
    m = y.shape[0]
    half = y.shape[1] // 2
    ns = half // TILE_LANES
    lo = pltpu.bitcast(y[:, :half].astype(BF16).astype(F32), jnp.uint32) >> 16
    hi = pltpu.bitcast(y[:, half:].astype(BF16).astype(F32), jnp.uint32) & jnp.uint32(0xFFFF0000)
    w = lo | hi
    for s in range(ns):
        ref[pl.ds(s, m, stride=ns), :] = w[:, s * TILE_LANES:(s + 1) * TILE_LANES]


def _unpack_rows(ref, s, m, ns):
    w = ref[pl.ds(s, m, stride=ns), :]
    lo = pltpu.bitcast(w << 16, F32)
    hi = pltpu.bitcast(w & jnp.uint32(0xFFFF0000), F32)
    return lo, hi


def _router_kernel(x_ref, g_ref, mod_ref, rw_ref, rb_ref, hp_ref, idx_ref, gate_ref, pos_ref, cnt_ref,
                   run_ref, *, tm):
    i = pl.program_id(0)

    @pl.when(i == 0)
    def _():
        run_ref[...] = jnp.zeros_like(run_ref)

    h = _modnorm(x_ref[...], g_ref[...], mod_ref[3:4, :], mod_ref[4:5, :])
    _pack_rows(hp_ref, h)
    logits = jnp.dot(h, rw_ref[...], preferred_element_type=F32, precision=lax.Precision.HIGHEST)
    scores = jax.nn.sigmoid(logits)
    n_e = scores.shape[1]
    lane = lax.broadcasted_iota(jnp.int32, scores.shape, 1).astype(F32)
    work = scores + rb_ref[...]
    sel = jnp.zeros_like(scores)
    hot, idx, gates = [], [], []
    for _ in range(TOP_K):
        m = jnp.max(work, axis=-1, keepdims=True)
        first = jnp.min(jnp.where(work == m, lane, float(n_e)), axis=-1, keepdims=True)
        one = lane == first
        work = jnp.where(one, -jnp.inf, work)
        hot.append(one)
        idx.append(first)
        gates.append(jnp.sum(jnp.where(one, scores, 0.0), axis=-1, keepdims=True))
        sel = sel + one.astype(F32)
    total = gates[0]
    for gj in gates[1:]:
        total = total + gj
    t_i = lax.broadcasted_iota(jnp.int32, (tm, tm), 0)
    s_i = lax.broadcasted_iota(jnp.int32, (tm, tm), 1)
    before = _dot((s_i < t_i).astype(BF16), sel.astype(BF16)) + run_ref[...]
    run_ref[...] = run_ref[...] + jnp.sum(sel, axis=0, keepdims=True)
    cnt_ref[...] = run_ref[...]
    slot = lax.broadcasted_iota(jnp.int32, (tm, SLOT), 1)
    idx_o = jnp.zeros((tm, SLOT), jnp.int32)
    gate_o = jnp.zeros((tm, SLOT), F32)
    pos_o = jnp.zeros((tm, SLOT), jnp.int32)
    for j in range(TOP_K):
        pos_j = jnp.sum(jnp.where(hot[j], before, 0.0), axis=-1, keepdims=True).astype(jnp.int32)
        idx_o = jnp.where(slot == j, idx[j].astype(jnp.int32), idx_o)
        gate_o = jnp.where(slot == j, gates[j] / total * ROUTED_SCALE, gate_o)
        pos_o = jnp.where(slot == j, pos_j, pos_o)
    idx_ref[...] = idx_o
    gate_ref[...] = gate_o
    pos_ref[...] = pos_o


def router(x, g, mod3, router_w, router_bias, seqs, *, tm=256):
    t, d = x.shape
    n_e = router_w.shape[1]
    ns = d // (2 * TILE_LANES)
    tok = lambda i: (i, 0)
    return pl.pallas_call(
        functools.partial(_router_kernel, tm=tm),
        grid=(t // tm,),
        in_specs=[pl.BlockSpec((tm, d), tok),
                  pl.BlockSpec((1, d), lambda i: (0, 0)),
                  pl.BlockSpec((None, 6, d), lambda i: (seqs.seq_of_tile(i, tm), 0, 0)),
                  pl.BlockSpec((d, n_e), lambda i: (0, 0)),
                  pl.BlockSpec((1, n_e), lambda i: (0, 0))],
        out_specs=[pl.BlockSpec((tm * ns, TILE_LANES), tok),
                   pl.BlockSpec((tm, SLOT), tok), pl.BlockSpec((tm, SLOT), tok), pl.BlockSpec((tm, SLOT), tok),
                   pl.BlockSpec((1, n_e), lambda i: (0, 0))],
        out_shape=[jax.ShapeDtypeStruct((t * ns, TILE_LANES), jnp.uint32),
                   jax.ShapeDtypeStruct((t, SLOT), jnp.int32),
                   jax.ShapeDtypeStruct((t, SLOT), F32),
                   jax.ShapeDtypeStruct((t, SLOT), jnp.int32),
                   jax.ShapeDtypeStruct((1, n_e), F32)],
        scratch_shapes=[pltpu.VMEM((1, n_e), F32)],
        name="moe_router",
        compiler_params=_params("arbitrary"),
    )(x, g, mod3, router_w, router_bias)


def _ffn_rows(xs_ref, wg_ref, wu_ref, wd_ref, xb_ref):
    bm = xb_ref.shape[0]
    half = xb_ref.shape[1] // 2
    ns = half // TILE_LANES
    for s in range(ns):
        lo, hi = _unpack_rows(xs_ref, s, bm, ns)
        xb_ref[:, s * TILE_LANES:(s + 1) * TILE_LANES] = lo.astype(BF16)
        xb_ref[:, half + s * TILE_LANES:half + (s + 1) * TILE_LANES] = hi.astype(BF16)
    xb = xb_ref[...]
    hg = _dot(xb, wg_ref[...])
    hu = _dot(xb, wu_ref[...])
    hb = (hg * jax.nn.sigmoid(hg) * hu).astype(BF16)
    return _dot(hb, wd_ref[...])


def _shared_dispatch_kernel(dest_ref, zblk_ref, hp_ref, wg_ref, wu_ref, wd_ref, o_ref, xs_ref, xb_ref, zbuf_ref,
                            sem, zsem, *, tm, ns, bm):
    @pl.when(pl.program_id(0) == 0)
    def _():
        zbuf_ref[...] = jnp.zeros_like(zbuf_ref)

        def zero_copy(k):
            row0 = pl.multiple_of(zblk_ref[k] * (bm * ns), bm * ns)
            return pltpu.make_async_copy(zbuf_ref, xs_ref.at[pl.ds(row0, bm * ns)], zsem)

        def zero_issue(k, carry):
            @pl.when(zblk_ref[k] >= 0)
            def _():
                zero_copy(k).start()
            return carry

        def zero_drain(k, carry):
            @pl.when(zblk_ref[k] >= 0)
            def _():
                zero_copy(k).wait()
            return carry

        lax.fori_loop(0, zblk_ref.shape[0], zero_issue, 0)
        lax.fori_loop(0, zblk_ref.shape[0], zero_drain, 0)

    def row_copy(t, j):
        src = hp_ref.at[pl.ds(pl.multiple_of(t * ns, ns), ns)]
        dst = xs_ref.at[pl.ds(pl.multiple_of(dest_ref[t * SLOT + j], ns), ns)]
        return pltpu.make_async_copy(src, dst, sem)

    def issue(t, carry):
        for j in range(TOP_K):
            row_copy(t, j).start(priority=j % 2)
        return carry

    def drain(t, carry):
        for j in range(TOP_K):
            row_copy(t, j).wait()
        return carry

    lax.fori_loop(0, tm, issue, 0, unroll=DMA_LOOP_UNROLL)
    o_ref[...] = _ffn_rows(hp_ref, wg_ref, wu_ref, wd_ref, xb_ref)
    lax.fori_loop(0, tm, drain, 0, unroll=DMA_LOOP_UNROLL)


def shared_ffn_dispatch(hp, dest_rows, zero_blocks, n_rows, w_a, w_b, w_down, *, bm, tm=256):
    f, d = w_down.shape
    ns = d // (2 * TILE_LANES)
    t = hp.shape[0] // ns
    const = lambda i: (0, 0)
    return pl.pallas_call(
        functools.partial(_shared_dispatch_kernel, tm=tm, ns=ns, bm=bm),
        grid=(t // tm,),
        in_specs=[pl.BlockSpec((tm * SLOT,), lambda i: (i,), memory_space=pltpu.SMEM),
                  pl.BlockSpec(memory_space=pltpu.SMEM),
                  pl.BlockSpec((tm * ns, TILE_LANES), lambda i: (i, 0)),
                  pl.BlockSpec(w_a.shape, const), pl.BlockSpec(w_b.shape, const), pl.BlockSpec((f, d), const)],
        out_specs=[pl.BlockSpec((tm, d), lambda i: (i, 0)), pl.BlockSpec(memory_space=pl.ANY)],
        out_shape=[jax.ShapeDtypeStruct((t, d), F32),
                   jax.ShapeDtypeStruct((n_rows * ns, TILE_LANES), jnp.uint32)],
        scratch_shapes=[pltpu.VMEM((tm, d), BF16), pltpu.VMEM((bm * ns, TILE_LANES), jnp.uint32),
                        pltpu.SemaphoreType.DMA(()), pltpu.SemaphoreType.DMA(())],
        name="moe_shared_expert_dispatch",
        compiler_params=_params("arbitrary"),
    )(dest_rows, zero_blocks, hp, w_a, w_b, w_down)


def _expert_kernel(be_ref, nblk_ref, xs_ref, wg_ref, wu_ref, wd_ref, o_ref, xb_ref):
    i = pl.program_id(0)

    @pl.when(i < nblk_ref[0])
    def _():
        _pack_rows(o_ref, _ffn_rows(xs_ref, wg_ref, wu_ref, wd_ref, xb_ref))

    @pl.when(i >= nblk_ref[0])
    def _():
        o_ref[...] = jnp.zeros_like(o_ref)


def expert_ffn(xs, blk_expert, n_used, w_a, w_b, w_down, *, bm):
    f, d = w_down.shape[1:]
    ns = d // (2 * TILE_LANES)
    n_rows = xs.shape[0] // ns
    rows = pl.BlockSpec((bm * ns, TILE_LANES), lambda i, be, nb: (i, 0))
    return pl.pallas_call(
        _expert_kernel,
        grid_spec=pltpu.PrefetchScalarGridSpec(
            num_scalar_prefetch=2,
            grid=(n_rows // bm,),
            in_specs=[rows,
                      pl.BlockSpec((None,) + w_a.shape[1:], lambda i, be, nb: (be[i], 0, 0)),
                      pl.BlockSpec((None,) + w_b.shape[1:], lambda i, be, nb: (be[i], 0, 0)),
                      pl.BlockSpec((None, f, d), lambda i, be, nb: (be[i], 0, 0))],
            out_specs=rows,
            scratch_shapes=[pltpu.VMEM((bm, d), BF16)]),
        out_shape=jax.ShapeDtypeStruct((n_rows * ns, TILE_LANES), jnp.uint32),
        name="moe_experts",
        compiler_params=_params("arbitrary"),
    )(blk_expert, n_used, xs, w_a, w_b, w_down)


def _combine_kernel(dest_ref, dest_next_ref, ys_ref, gate_ref, sh_ref, x_ref, mod_ref, o_ref, buf_ref, sems,
                    *, tm, ns):
    i = pl.program_id(0)
    n = pl.num_programs(0)
    slot = lax.rem(i, 2)

    def row_copy(dref, t, j, sl):
        src = ys_ref.at[pl.ds(pl.multiple_of(dref[t * SLOT + j], ns), ns)]
        dst = buf_ref.at[sl * TOP_K + j, pl.ds(pl.multiple_of(t * ns, ns), ns)]
        return pltpu.make_async_copy(src, dst, sems.at[sl])

    def gather(dref, sl):
        def issue(t, carry):
            for j in range(TOP_K):
                row_copy(dref, t, j, sl).start()
            return carry
        lax.fori_loop(0, tm, issue, 0, unroll=DMA_LOOP_UNROLL)

    @pl.when(i == 0)
    def _():
        gather(dest_ref, 0)

    @pl.when(i + 1 < n)
    def _():
        gather(dest_next_ref, 1 - slot)

    def drain(t, carry):
        for j in range(TOP_K):
            row_copy(dest_ref, t, j, slot).wait()
        return carry

    lax.fori_loop(0, tm, drain, 0, unroll=DMA_LOOP_UNROLL)
    gates = gate_ref[...]
    gj = [jnp.broadcast_to(gates[:, j:j + 1], (tm, TILE_LANES)) for j in range(TOP_K)]
    gate_f = mod_ref[5:6, :]
    half = x_ref.shape[1] // 2
    for s in range(ns):
        acc_lo = jnp.zeros((tm, TILE_LANES), F32)
        acc_hi = jnp.zeros((tm, TILE_LANES), F32)
        for j in range(TOP_K):
            lo, hi = _unpack_rows(buf_ref.at[slot * TOP_K + j], s, tm, ns)
            acc_lo = acc_lo + gj[j] * lo
            acc_hi = acc_hi + gj[j] * hi
        for acc, off in ((acc_lo, s * TILE_LANES), (acc_hi, half + s * TILE_LANES)):
            sl = slice(off, off + TILE_LANES)
            o_ref[:, sl] = x_ref[:, sl] + gate_f[:, sl] * (acc + sh_ref[:, sl])


def combine(ys, dest_rows, gates, shared, x, mod3, seqs, *, tm=128):
    t, d = x.shape
    ns = d // (2 * TILE_LANES)
    n = t // tm
    tok = lambda i: (i, 0)
    return pl.pallas_call(
        functools.partial(_combine_kernel, tm=tm, ns=ns),
        grid=(n,),
        in_specs=[pl.BlockSpec((tm * SLOT,), lambda i: (i,), memory_space=pltpu.SMEM),
                  pl.BlockSpec((tm * SLOT,), lambda i: (jnp.minimum(i + 1, n - 1),), memory_space=pltpu.SMEM),
                  pl.BlockSpec(memory_space=pl.ANY),
                  pl.BlockSpec((tm, SLOT), tok),
                  pl.BlockSpec((tm, d), tok),
                  pl.BlockSpec((tm, d), tok),
                  pl.BlockSpec((None, 6, d), lambda i: (seqs.seq_of_tile(i, tm), 0, 0))],
        out_specs=pl.BlockSpec((tm, d), tok),
        out_shape=jax.ShapeDtypeStruct((t, d), F32),
        scratch_shapes=[pltpu.VMEM((2 * TOP_K, tm * ns, TILE_LANES), jnp.uint32),
                        pltpu.SemaphoreType.DMA((2,))],
        name="moe_combine",
        compiler_params=_params("arbitrary"),
    )(dest_rows, dest_rows, ys, gates, shared, x, mod3)


def moe_block(x, mod3, g, router_w, router_bias, w_gate, w_up, w_down, sh_gate, sh_up, sh_down, layer, seqs,
              *, bm=256):
    t, d = x.shape
    n_e = router_w.shape[1]
    hp, idx, gates, pos, counts = router(x, g, mod3, router_w, router_bias, seqs)
    cnt = counts[0].astype(jnp.int32)
    padded = (cnt + bm - 1) // bm * bm
    end_pad = jnp.cumsum(padded)
    start_pad = end_pad - padded
    ns = d // (2 * TILE_LANES)
    dest_rows = ((jnp.take(start_pad, idx, axis=0) + pos) * ns).reshape(-1)
    nblk = -(-(t * TOP_K) // bm) + n_e
    blk_start = jnp.arange(nblk, dtype=jnp.int32) * bm
    blk_expert = jnp.minimum(jnp.sum(end_pad[None, :] <= blk_start[:, None], axis=1), n_e - 1).astype(jnp.int32)
    n_used = (end_pad[-1:] // bm).astype(jnp.int32)
    last_blk = jnp.where(padded > 0, end_pad // bm - 1, -1)
    tail_blk = n_used[0] + jnp.arange(n_e, dtype=jnp.int32)
    zero_blocks = jnp.concatenate([last_blk, jnp.where(tail_blk < nblk, tail_blk, -1)]).astype(jnp.int32)
    sh = lambda w: cast_bf16(w[:, None], layer)[0]
    shared, xs = shared_ffn_dispatch(hp, dest_rows, zero_blocks, nblk * bm, sh(sh_gate), sh(sh_up), sh(sh_down),
                                     bm=bm)
    bf = lambda w: cast_bf16(w, layer)
    ys = expert_ffn(xs, blk_expert, n_used, bf(w_gate), bf(w_up), bf(w_down), bm=bm)
    return combine(ys, dest_rows, gates, shared, x, mod3, seqs)


def _final_norm_kernel(x_ref, g_ref, o_ref):
    x = x_ref[...]
    o_ref[...] = x * lax.rsqrt(jnp.mean(x * x, axis=-1, keepdims=True) + RMS_EPS) * g_ref[...]


def final_rmsnorm(x, g, row0, rows, *, tm=256):
    d = x.shape[1]
    b0 = row0 // tm
    return pl.pallas_call(
        _final_norm_kernel,
        grid=(rows // tm,),
        in_specs=[pl.BlockSpec((tm, d), lambda i: (b0 + i, 0)), pl.BlockSpec((1, d), lambda i: (0, 0))],
        out_specs=pl.BlockSpec((tm, d), lambda i: (i, 0)),
        out_shape=jax.ShapeDtypeStruct((rows, d), F32),
        name="final_norm",
        compiler_params=_params("parallel"),
    )(x, g)


def _block_diag2(w):
    k, n = w.shape[1:]
    z = jnp.zeros((k, n), w.dtype)
    return jnp.concatenate([jnp.concatenate([w[0], z], axis=1), jnp.concatenate([z, w[1]], axis=1)], axis=0)


def rwkv_block(x, mod3, g, seqs, mu, w_r, w_k, w_v, w_o, dec_w0, dec_w1, dec_w2, iclr_a0, iclr_a1, iclr_a2,
               gate_g1, gate_g2, k_k, k_a, r_k, gn_w, gn_b):
    d = x.shape[1]
    bf = lambda w: w.astype(BF16)
    xr, xw, xk, xv, xa, xg = shiftmix(x, g, mod3, mu, seqs)
    r = mm(xr, bf(w_r), out_dtype=F32, name="rwkv_proj")
    k = mm(xk, bf(w_k), out_dtype=F32, name="rwkv_proj")
    v = mm(xv, bf(w_v), out_dtype=F32, name="rwkv_proj")
    lw_in = mm(xw, bf(jnp.concatenate([dec_w1[0], dec_w1[1]], axis=1)), out_dtype=BF16, epi=_epi_tanh,
               name="rwkv_decay_lora_in")
    la_in = mm(xa, bf(jnp.concatenate([iclr_a1[0], iclr_a1[1]], axis=1)), out_dtype=BF16, name="rwkv_iclr_lora_in")
    lg_in = mm(xg, bf(gate_g1), out_dtype=BF16, epi=_epi_sigmoid, name="rwkv_gate_lora_in")
    gate = mm(lg_in, bf(gate_g2), out_dtype=F32, name="rwkv_gate_lora_out")
    lw2 = mm(lw_in, bf(_block_diag2(dec_w2)), out_dtype=F32, epi=_epi_log_decay,
             extras=[(dec_w0.reshape(1, 2 * d), "col")], name="rwkv_log_decay")
    a2 = mm(la_in, bf(_block_diag2(iclr_a2)), out_dtype=F32, epi=_epi_bias_sigmoid,
            extras=[(iclr_a0.reshape(1, 2 * d), "col")], name="rwkv_iclr")
    k_a2 = k_a.reshape(1, d)
    yf, yb = wkv(r, k, v, lw2, a2, k_k.reshape(1, d), k_a2, seqs)
    yo = rwkv_post(yf, yb, r, k, v, a2, gate, k_a2, r_k.reshape(1, d), gn_w.reshape(1, d), gn_b.reshape(1, d))
    return mm(yo, bf(w_o), out_dtype=F32, epi=_epi_residual(2), extras=[(x, "row"), (mod3, "seq")], seqs=seqs,
              name="mixer_out_proj")


def attention_block(x, mod3, g, seqs, w_qkv, w_o, sink):
    d = x.shape[1]
    h = modnorm(x, g, mod3, seqs, row=0)
    qkv = mm(h, w_qkv.astype(BF16), out_dtype=BF16, name="attn_qkv")
    o = window_attention(qkv, sink, seqs, d=d)
    return mm(o, w_o.astype(BF16), out_dtype=F32, epi=_epi_residual(2), extras=[(x, "row"), (mod3, "seq")],
              seqs=seqs, name="mixer_out_proj")


def kernel(x_prompt, x_sample, c_prompt, c_sample, ada_w, ada_b, norm_mix, norm_ffn, rw_mu, rw_wr, rw_wk, rw_wv, rw_wo, rw_dec_w0, rw_dec_w1, rw_dec_w2, rw_iclr_a0, rw_iclr_a1, rw_iclr_a2, rw_gate_g1, rw_gate_g2, rw_k_k, rw_k_a, rw_r_k, rw_gn_w, rw_gn_b, at_w_qkv, at_w_o, at_sink, router_w, router_bias, moe_w_gate, moe_w_up, moe_w_down, sh_w_gate, sh_w_up, sh_w_down, final_norm):
    b1, s1, d = x_prompt.shape
    b2, s2, _ = x_sample.shape
    seqs = Seqs(b1, s1, b2, s2)
    depth = ada_w.shape[0]
    x = jnp.concatenate([x_prompt.reshape(b1 * s1, d), x_sample.reshape(b2 * s2, d)], axis=0)
    c = jnp.concatenate([c_prompt, c_sample], axis=0)
    rows = -(-seqs.nseq // TILE_SUBLANES) * TILE_SUBLANES
    c_act = jnp.pad(jax.nn.silu(c), ((0, rows - seqs.nseq), (0, 0)))
    for i in range(depth):
        j = i // 2
        mod = mm(c_act, ada_w[i], out_dtype=F32, epi=_epi_bias, extras=[(ada_b[i].reshape(1, -1), "col")],
                 nb=d, precision=lax.Precision.HIGHEST, name="adaln_mod")
        mod3 = mod[:seqs.nseq].reshape(seqs.nseq, 6, d)
        g_mix = norm_mix[i].reshape(1, d)
        if i % 2 == 0:
            x = rwkv_block(x, mod3, g_mix, seqs, rw_mu[j], rw_wr[j], rw_wk[j], rw_wv[j], rw_wo[j], rw_dec_w0[j],
                           rw_dec_w1[j], rw_dec_w2[j], rw_iclr_a0[j], rw_iclr_a1[j], rw_iclr_a2[j],
                           rw_gate_g1[j], rw_gate_g2[j], rw_k_k[j], rw_k_a[j], rw_r_k[j], rw_gn_w[j], rw_gn_b[j])
        else:
            x = attention_block(x, mod3, g_mix, seqs, at_w_qkv[j], at_w_o[j], at_sink[j])
        x = moe_block(x, mod3, norm_ffn[i].reshape(1, d), router_w[i], router_bias[i].reshape(1, -1),
                      moe_w_gate, moe_w_up, moe_w_down, sh_w_gate, sh_w_up, sh_w_down, i, seqs)
    fw = final_norm.reshape(1, d)
    y1 = final_rmsnorm(x, fw, 0, seqs.t1).reshape(b1, s1, d)
    y2 = final_rmsnorm(x, fw, seqs.t1, b2 * s2).reshape(b2, s2, d)
    return (y1, y2)
```

```python
import functools
from typing import NamedTuple

import jax
import jax.numpy as jnp
from jax import lax
from jax.experimental import pallas as pl
from jax.experimental.pallas import tpu as pltpu

F32 = jnp.float32
BF16 = jnp.bfloat16

HEAD = 64
GROUP_LANES = 256
CHUNK = 64
ATT_GROUP = 4
ATT_KV_BATCH = 1
WINDOW = 128
ATT_BLOCK = 128
TOP_K = 6
ROUTED_SCALE = 2.446
RMS_EPS = 1e-6
GN_EPS = 64e-5
NEG_INF = -1e30
SLOT = 8
DMA_LOOP_UNROLL = 8
TILE_SUBLANES = 8
TILE_LANES = 128
VMEM_LIMIT = 56 * 1024 * 1024


class Seqs(NamedTuple):
    b1: int
    s1: int
    b2: int
    s2: int

    @property
    def t1(self):
        return self.b1 * self.s1

    @property
    def total(self):
        return self.b1 * self.s1 + self.b2 * self.s2

    @property
    def nseq(self):
        return self.b1 + self.b2

    def seq_of_tile(self, i, tm):
        n1 = self.t1 // tm
        return jnp.where(i < n1, i // (self.s1 // tm), self.b1 + (i - n1) // (self.s2 // tm))

    def starts_seq(self, tok):
        return jnp.where(tok < self.t1, lax.rem(tok, self.s1) == 0, lax.rem(tok - self.t1, self.s2) == 0)

    def ends_seq(self, tok_end):
        return jnp.where(tok_end <= self.t1, lax.rem(tok_end, self.s1) == 0,
                         lax.rem(tok_end - self.t1, self.s2) == 0)


def _params(*sem):
    return pltpu.CompilerParams(dimension_semantics=sem, vmem_limit_bytes=VMEM_LIMIT)


def _modnorm(x, g, shift, scale):
    y = x * lax.rsqrt(jnp.mean(x * x, axis=-1, keepdims=True) + RMS_EPS)
    return (y * g) * (1.0 + scale) + shift


def _split_bf16(x):
    hi = x.astype(BF16)
    lo = (x - hi.astype(F32)).astype(BF16)
    return hi, lo


def _dot(a, b):
    return jnp.dot(a, b, preferred_element_type=F32)


def _dot_nt(a, b):
    return lax.dot_general(a, b, (((1,), (1,)), ((), ())), preferred_element_type=F32)


def _dot_tn(a, b):
    return lax.dot_general(a, b, (((0,), (0,)), ((), ())), preferred_element_type=F32)


def _block_mask():
    r = lax.broadcasted_iota(jnp.int32, (GROUP_LANES, GROUP_LANES), 0) // HEAD
    c = lax.broadcasted_iota(jnp.int32, (GROUP_LANES, GROUP_LANES), 1) // HEAD
    return r == c


def _group_sum(x, ones_bd):
    hi, lo = _split_bf16(x)
    return _dot(hi, ones_bd) + _dot(lo, ones_bd)


def _modnorm_kernel(x_ref, g_ref, mod_ref, o_ref, *, row):
    h = _modnorm(x_ref[...], g_ref[...], mod_ref[row:row + 1, :], mod_ref[row + 1:row + 2, :])
    o_ref[...] = h.astype(o_ref.dtype)


def modnorm(x, g, mod3, seqs, *, row, tm=256):
    t, d = x.shape
    return pl.pallas_call(
        functools.partial(_modnorm_kernel, row=row),
        grid=(t // tm,),
        in_specs=[pl.BlockSpec((tm, d), lambda i: (i, 0)),
                  pl.BlockSpec((1, d), lambda i: (0, 0)),
                  pl.BlockSpec((None, 6, d), lambda i: (seqs.seq_of_tile(i, tm), 0, 0))],
        out_specs=pl.BlockSpec((tm, d), lambda i: (i, 0)),
        out_shape=jax.ShapeDtypeStruct((t, d), BF16),
        name="modnorm",
        compiler_params=_params("parallel"),
    )(x, g, mod3)


def _shiftmix_kernel(x_ref, xp_ref, xn_ref, g_ref, mod_ref, mu_ref, *o_refs, seqs, tm):
    i = pl.program_id(0)
    g = g_ref[...]
    shift = mod_ref[0:1, :]
    scale = mod_ref[1:2, :]
    h = _modnorm(x_ref[...], g, shift, scale)
    hp = _modnorm(xp_ref[...], g, shift, scale)[TILE_SUBLANES - 1:TILE_SUBLANES, :]
    hn = _modnorm(xn_ref[...], g, shift, scale)[0:1, :]
    hp = jnp.where(seqs.starts_seq(i * tm), 0.0, hp)
    hn = jnp.where(seqs.ends_seq((i + 1) * tm), 0.0, hn)
    rows = lax.broadcasted_iota(jnp.int32, h.shape, 0)
    prev = jnp.where(rows == 0, hp, pltpu.roll(h, 1, 0))
    nxt = jnp.where(rows == tm - 1, hn, pltpu.roll(h, tm - 1, 0))
    xx = 0.5 * (prev + nxt) - h
    for j, o_ref in enumerate(o_refs):
        o_ref[...] = (h + xx * mu_ref[j:j + 1, :]).astype(o_ref.dtype)


def shiftmix(x, g, mod3, mu, seqs, *, tm=256):
    t, d = x.shape
    nb8 = t // TILE_SUBLANES
    r8 = tm // TILE_SUBLANES
    return pl.pallas_call(
        functools.partial(_shiftmix_kernel, seqs=seqs, tm=tm),
        grid=(t // tm,),
        in_specs=[pl.BlockSpec((tm, d), lambda i: (i, 0)),
                  pl.BlockSpec((TILE_SUBLANES, d), lambda i: (jnp.maximum(i * r8 - 1, 0), 0)),
                  pl.BlockSpec((TILE_SUBLANES, d), lambda i: (jnp.minimum((i + 1) * r8, nb8 - 1), 0)),
                  pl.BlockSpec((1, d), lambda i: (0, 0)),
                  pl.BlockSpec((None, 6, d), lambda i: (seqs.seq_of_tile(i, tm), 0, 0)),
                  pl.BlockSpec((6, d), lambda i: (0, 0))],
        out_specs=[pl.BlockSpec((tm, d), lambda i: (i, 0))] * 6,
        out_shape=[jax.ShapeDtypeStruct((t, d), BF16)] * 6,
        name="rwkv_shiftmix",
        compiler_params=_params("parallel"),
    )(x, x, x, g, mod3, mu)


def _cast_kernel(w_ref, o_ref):
    o_ref[...] = w_ref[...].astype(o_ref.dtype)


def cast_bf16(w, layer):
    _, e, k, n = w.shape
    return pl.pallas_call(
        _cast_kernel,
        grid=(e,),
        in_specs=[pl.BlockSpec((None, None, k, n), lambda i: (layer, i, 0, 0))],
        out_specs=pl.BlockSpec((None, k, n), lambda i: (i, 0, 0)),
        out_shape=jax.ShapeDtypeStruct((e, k, n), BF16),
        name="cast_bf16",
        compiler_params=_params("parallel"),
    )(w)


def _mm_kernel(a_ref, w_ref, *refs, epi, tn, n_extra, precision):
    extra = refs[:n_extra]
    o_ref = refs[n_extra]
    n = o_ref.shape[-1]
    a = a_ref[...]
    for c in range(n // tn):
        sl = slice(c * tn, (c + 1) * tn)
        acc = jnp.dot(a, w_ref[:, sl], preferred_element_type=F32, precision=precision)
        if epi is not None:
            acc = epi(acc, *[e[:, sl] for e in extra])
        o_ref[:, sl] = acc.astype(o_ref.dtype)


def mm(a, w, *, out_dtype, name, epi=None, extras=(), seqs=None, tm=512, tn=512, nb=None, precision=None):
    m, k = a.shape
    n = w.shape[1]
    nb = n if nb is None else nb
    tn = max([c for c in range(TILE_LANES, min(tn, nb) + 1, TILE_LANES) if nb % c == 0], default=nb)
    tm = min(tm, m)
    specs = [pl.BlockSpec((tm, k), lambda i, j: (i, 0)), pl.BlockSpec((k, nb), lambda i, j: (0, j))]
    args = [a, w]
    for arr, kind in extras:
        if kind == "col":
            specs.append(pl.BlockSpec((1, nb), lambda i, j: (0, j)))
        elif kind == "row":
            specs.append(pl.BlockSpec((tm, nb), lambda i, j: (i, j)))
        else:
            specs.append(pl.BlockSpec((None, arr.shape[1], nb),
                                      lambda i, j: (seqs.seq_of_tile(i, tm), 0, j)))
        args.append(arr)
    return pl.pallas_call(
        functools.partial(_mm_kernel, epi=epi, tn=tn, n_extra=len(extras), precision=precision),
        grid=(m // tm, n // nb),
        in_specs=specs,
        out_specs=pl.BlockSpec((tm, nb), lambda i, j: (i, j)),
        out_shape=jax.ShapeDtypeStruct((m, n), out_dtype),
        name=name,
        compiler_params=_params("parallel", "arbitrary"),
    )(*args)


def _epi_bias(acc, b):
    return acc + b


def _epi_tanh(acc):
    return jnp.tanh(acc)


def _epi_sigmoid(acc):
    return jax.nn.sigmoid(acc)


def _epi_log_decay(acc, w0):
    y = -(acc + w0)
    softplus = jnp.maximum(y, 0.0) + jnp.log1p(jnp.exp(-jnp.abs(y)))
    return -jnp.exp(-softplus - 0.5)


def _epi_bias_sigmoid(acc, b):
    return jax.nn.sigmoid(acc + b)


def _epi_residual(row):
    def epi(acc, x, mod):
        return x + mod[row:row + 1, :] * acc
    return epi


def _expand_bd(m, bm):
    mb = m.astype(BF16)
    return jnp.where(bm, jnp.concatenate([mb] * (GROUP_LANES // HEAD), axis=0), jnp.zeros((), BF16))


def _wkv_chunks(chains, bm, ones_bd):
    c = CHUNK
    n = len(chains)
    rev = [ch[8] for ch in chains]
    t_i = lax.broadcasted_iota(jnp.int32, (c, c), 0)
    s_i = lax.broadcasted_iota(jnp.int32, (c, c), 1)
    tri = {False: (s_i <= t_i).astype(BF16), True: (s_i >= t_i).astype(BF16)}
    t2 = lax.broadcasted_iota(jnp.int32, (c, GROUP_LANES), 0)
    s2 = lax.broadcasted_iota(jnp.int32, (c, GROUP_LANES), 1) % c
    strict = {False: s2 < t2, True: s2 > t2}
    incl = {False: s2 <= t2, True: s2 >= t2}
    eye = jnp.where(s2 == t2, 1.0, 0.0)

    k2 = [ch[1] * ch[5] for ch in chains]
    sq = [piece for q in k2 for piece in _split_bf16(q * q)]
    ss_all = _dot(jnp.concatenate(sq, axis=0), ones_bd)
    ss = [ss_all[2 * j * c:(2 * j + 1) * c] + ss_all[(2 * j + 1) * c:(2 * j + 2) * c] for j in range(n)]
    kk = [q / jnp.maximum(jnp.sqrt(s), 1e-12) for q, s in zip(k2, ss)]
    kd = [ch[1] * (1.0 + (ch[4] - 1.0) * ch[6]) for ch in chains]
    cw = [None] * n
    for rv in (False, True):
        ids = [j for j in range(n) if rev[j] == rv]
        if ids:
            lw = jnp.concatenate([piece for j in ids for piece in _split_bf16(chains[j][3])], axis=1)
            cw_all = _dot(tri[rv], lw)
            for pos, j in enumerate(ids):
                base = 2 * pos * GROUP_LANES
                cw[j] = (cw_all[:, base:base + GROUP_LANES]
                         + cw_all[:, base + GROUP_LANES:base + 2 * GROUP_LANES])
    edge = [w[0:1, :] if rv else w[c - 1:c, :] for w, rv in zip(cw, rev)]
    e_neg = [jnp.exp(-w) for w in cw]
    al = [-q * jnp.exp(w - ch[3]) for q, w, ch in zip(kk, cw, chains)]
    rb = [ch[0] * jnp.exp(w) for w, ch in zip(cw, chains)]
    ka = [q * ch[4] for q, ch in zip(kk, chains)]
    be = [q * e for q, e in zip(ka, e_neg)]
    kb = [q * e for q, e in zip(kd, e_neg)]
    x = [jnp.concatenate([p, q], axis=0).astype(BF16) for p, q in zip(al, rb)]
    z_bd = [jnp.concatenate([_expand_bd(p, bm), _expand_bd(q, bm)], axis=0) for p, q in zip(be, kb)]
    a_all = [_dot_nt(p, q) for p, q in zip(x, z_bd)]
    a_ab = [jnp.where(strict[rv], m[:c, :GROUP_LANES], 0.0) for m, rv in zip(a_all, rev)]
    a_kr = [jnp.concatenate([jnp.where(strict[rv], m[:c, GROUP_LANES:], 0.0),
                             jnp.where(incl[rv], m[c:, GROUP_LANES:], 0.0)], axis=0).astype(BF16)
            for m, rv in zip(a_all, rev)]
    a_rb = [jnp.where(incl[rv], m[c:, :GROUP_LANES], 0.0).astype(BF16) for m, rv in zip(a_all, rev)]
    p = [eye + m for m in a_ab]
    ap = [_dot(m.astype(BF16), _expand_bd(m, bm)) for m in a_ab]
    for _ in range(4):
        both = [_dot(jnp.concatenate([m, q], axis=0).astype(BF16), _expand_bd(m, bm)) for m, q in zip(ap, p)]
        p = [q + b[c:] for q, b in zip(p, both)]
        ap = [b[:c] for b in both]
    p = [q + _dot(q.astype(BF16), _expand_bd(m, bm)) for q, m in zip(p, ap)]
    xh = [_dot_nt(q, ch[7].astype(BF16)) for q, ch in zip(x, chains)]
    av = [_dot(m, _expand_bd(ch[2], bm)) for m, ch in zip(a_kr, chains)]
    u = [_dot(q.astype(BF16), _expand_bd(h[:c] + w[:c], bm)) for q, h, w in zip(p, xh, av)]
    y = [h[c:] + w[c:] + _dot(m, _expand_bd(q, bm)) for h, w, m, q in zip(xh, av, a_rb, u)]
    e_rest = [jnp.exp(e - w) for e, w in zip(edge, cw)]
    zw = [jnp.concatenate([p_ * e, q * e], axis=0).astype(BF16) for p_, q, e in zip(ka, kd, e_rest)]
    uv = [jnp.concatenate([q, ch[2]], axis=0).astype(BF16) for q, ch in zip(u, chains)]
    upd = [_dot_tn(p_, q) for p_, q in zip(uv, zw)]
    ht_new = [chains[j][7] * jnp.exp(edge[j]) + jnp.where(bm, upd[j], 0.0) for j in range(n)]
    return y, ht_new


def _wkv_kernel(rf, kf, vf, lwf, af, rb, kb, vb, lwb, ab, kk_ref, ka_ref, yf_ref, yb_ref, st_ref,
                *, seqs, ct, ngb):
    i = pl.program_id(1)
    nt = pl.num_programs(1)
    nc = ct // CHUNK
    bm = _block_mask()
    ones_bd = bm.astype(BF16)

    def load(refs, lanes, off, st_idx, reset, reverse):
        rows = pl.ds(off, CHUNK)
        r, k, v, lw, a = [ref[rows, lanes] for ref in refs]
        ht = jnp.where(reset, 0.0, st_ref[st_idx])
        return (r, k, v, lw, a, kk_ref[:, lanes], ka_ref[:, lanes], ht, reverse)

    def body(c, carry):
        off_f = pl.multiple_of(c * CHUNK, CHUNK)
        off_b = pl.multiple_of((nc - 1 - c) * CHUNK, CHUNK)
        tok_b = (nt - 1 - i) * ct + off_b
        reset_f = seqs.starts_seq(i * ct + off_f)
        reset_b = seqs.ends_seq(tok_b + CHUNK)
        lanes = [slice(g * GROUP_LANES, (g + 1) * GROUP_LANES) for g in range(ngb)]
        chains = [load((rf, kf, vf, lwf, af), lanes[g], off_f, g, reset_f, False) for g in range(ngb)]
        chains += [load((rb, kb, vb, lwb, ab), lanes[g], off_b, ngb + g, reset_b, True) for g in range(ngb)]
        y, ht = _wkv_chunks(chains, bm, ones_bd)
        for g in range(ngb):
            st_ref[g] = ht[g]
            st_ref[ngb + g] = ht[ngb + g]
            yf_ref[pl.ds(off_f, CHUNK), lanes[g]] = y[g]
            yb_ref[pl.ds(off_b, CHUNK), lanes[g]] = y[ngb + g]
        return carry

    lax.fori_loop(0, nc, body, 0)


def wkv(r, k, v, lw2, a2, k_k, k_a, seqs, *, ct=128, ngb=8):
    t, d = r.shape
    ngb = min(ngb, d // GROUP_LANES)
    wl = ngb * GROUP_LANES
    nlb = d // wl
    nt = t // ct
    fwd = lambda g, i: (i, g)
    bwd = lambda g, i: (nt - 1 - i, g)
    bwd2 = lambda g, i: (nt - 1 - i, nlb + g)
    blk = (ct, wl)
    par = pl.BlockSpec((1, wl), lambda g, i: (0, g))
    return pl.pallas_call(
        functools.partial(_wkv_kernel, seqs=seqs, ct=ct, ngb=ngb),
        grid=(nlb, nt),
        in_specs=[pl.BlockSpec(blk, fwd)] * 5 + [pl.BlockSpec(blk, bwd)] * 3 + [pl.BlockSpec(blk, bwd2)] * 2
                 + [par, par],
        out_specs=[pl.BlockSpec(blk, fwd), pl.BlockSpec(blk, bwd)],
        out_shape=[jax.ShapeDtypeStruct((t, d), F32)] * 2,
        scratch_shapes=[pltpu.VMEM((2 * ngb, GROUP_LANES, GROUP_LANES), F32)],
        name="wkv_scan",
        compiler_params=_params("parallel", "arbitrary"),
    )(r, k, v, lw2, a2, r, k, v, lw2, a2, k_k, k_a)


def _rwkv_post_kernel(yf, yb, r, k, v, af, ab, g, ka, rk, gw, gb, o_ref):
    ones_bd = _block_mask().astype(BF16)
    inv_n = 1.0 / HEAD
    for c in range(o_ref.shape[1] // GROUP_LANES):
        sl = slice(c * GROUP_LANES, (c + 1) * GROUP_LANES)
        y = yf[:, sl] + yb[:, sl]
        mean = _group_sum(y, ones_bd) * inv_n
        yc = y - mean
        var = _group_sum(yc * yc, ones_bd) * inv_n
        yn = yc * lax.rsqrt(var + GN_EPS) * gw[:, sl] + gb[:, sl]
        kv = k[:, sl]
        k_a = ka[:, sl]
        kd_sum = kv * (1.0 + (af[:, sl] - 1.0) * k_a) + kv * (1.0 + (ab[:, sl] - 1.0) * k_a)
        bonus = _group_sum(r[:, sl] * kd_sum * rk[:, sl], ones_bd) * v[:, sl]
        o_ref[:, sl] = ((yn + bonus) * g[:, sl]).astype(o_ref.dtype)


def rwkv_post(yf, yb, r, k, v, a2, g, k_a, r_k, gn_w, gn_b, *, tm=512, lanes=1024):
    t, d = r.shape
    lanes = min(lanes, d)
    ng = d // lanes
    blk = pl.BlockSpec((tm, lanes), lambda i, j: (i, j))
    blk_b = pl.BlockSpec((tm, lanes), lambda i, j: (i, ng + j))
    par = pl.BlockSpec((1, lanes), lambda i, j: (0, j))
    return pl.pallas_call(
        _rwkv_post_kernel,
        grid=(t // tm, ng),
        in_specs=[blk] * 5 + [blk, blk_b, blk] + [par] * 4,
        out_specs=blk,
        out_shape=jax.ShapeDtypeStruct((t, d), BF16),
        name="rwkv_post",
        compiler_params=_params("parallel", "parallel"),
    )(yf, yb, r, k, v, a2, a2, g, k_a, r_k, gn_w, gn_b)


def _attn_kernel(sink_ref, q_ref, kp_ref, kc_ref, kn_ref, vp_ref, vc_ref, vn_ref, o_ref, *, seqs, n_heads):
    i = pl.program_id(0)
    blk = ATT_BLOCK
    span = blk + 2 * WINDOW
    first = seqs.starts_seq(i * blk)
    last = seqs.ends_seq((i + 1) * blk)
    qi = lax.broadcasted_iota(jnp.int32, (blk, span), 0)
    ci = lax.broadcasted_iota(jnp.int32, (blk, span), 1)
    dist_i = jnp.abs(qi - (ci - WINDOW))
    valid = (dist_i <= WINDOW) & ((ci >= WINDOW) | jnp.logical_not(first)) \
        & ((ci < WINDOW + blk) | jnp.logical_not(last))
    dist = dist_i.astype(F32)
    n_kv = n_heads // ATT_GROUP
    kv_batch = min(ATT_KV_BATCH, n_kv)
    outs = []
    for kv0 in range(0, n_kv, kv_batch):
        heads = [(kvh, kvh * ATT_GROUP + g) for kvh in range(kv0, kv0 + kv_batch) for g in range(ATT_GROUP)]
        k_span, v_span = {}, {}
        for kvh in range(kv0, kv0 + kv_batch):
            ks = slice(kvh * HEAD, (kvh + 1) * HEAD)
            k_span[kvh] = jnp.concatenate([kp_ref[:, ks], kc_ref[:, ks], kn_ref[:, ks]], axis=0)
            v_span[kvh] = jnp.concatenate([vp_ref[:, ks], vc_ref[:, ks], vn_ref[:, ks]], axis=0)
        s = [_dot_nt(q_ref[:, h * HEAD:(h + 1) * HEAD], k_span[kvh]) for kvh, h in heads]
        s = [jnp.where(valid, sc * (HEAD ** -0.5) - (2.0 ** (-8.0 * (h + 1) / n_heads)) * dist, NEG_INF)
             for sc, (_, h) in zip(s, heads)]
        m = [jnp.maximum(jnp.max(sc, axis=-1, keepdims=True), sink_ref[h]) for sc, (_, h) in zip(s, heads)]
        e = [jnp.exp(sc - mx) for sc, mx in zip(s, m)]
        denom = [jnp.sum(ex, axis=-1, keepdims=True) + jnp.exp(sink_ref[h] - mx)
                 for ex, mx, (_, h) in zip(e, m, heads)]
        pv = [_dot(ex.astype(BF16), v_span[kvh]) for ex, (kvh, _) in zip(e, heads)]
        outs += [o / dn for o, dn in zip(pv, denom)]
    o_ref[...] = jnp.concatenate(outs, axis=-1).astype(o_ref.dtype)


def window_attention(qkv, sink, seqs, *, d):
    t = qkv.shape[0]
    n_heads = d // HEAD
    kvw = (n_heads // ATT_GROUP) * HEAD
    nb = t // ATT_BLOCK
    kcol = d // kvw
    prev = lambda i: jnp.maximum(i - 1, 0)
    nxt = lambda i: jnp.minimum(i + 1, nb - 1)
    kv_specs = [pl.BlockSpec((ATT_BLOCK, kvw), lambda i, c=c, f=f: (f(i), c))
                for c in (kcol, kcol + 1) for f in (prev, lambda i: i, nxt)]
    return pl.pallas_call(
        functools.partial(_attn_kernel, seqs=seqs, n_heads=n_heads),
        grid=(nb,),
        in_specs=[pl.BlockSpec(memory_space=pltpu.SMEM),
                  pl.BlockSpec((ATT_BLOCK, d), lambda i: (i, 0))] + kv_specs,
        out_specs=pl.BlockSpec((ATT_BLOCK, d), lambda i: (i, 0)),
        out_shape=jax.ShapeDtypeStruct((t, d), BF16),
        name="window_attention",
        compiler_params=_params("parallel"),
    )(sink, qkv, qkv, qkv, qkv, qkv, qkv, qkv)


def _pack_rows(ref, y):
    m = y.shape[0]
    half = y.shape[1] // 2
    ns = half // TILE_LANES
    lo = pltpu.bitcast(y[:, :half].astype(BF16).astype(F32), jnp.uint32) >> 16
    hi = pltpu.bitcast(y[:, half:].astype(BF16).astype(F32), jnp.uint32) & jnp.uint32(0xFFFF0000)
    w = lo | hi
    for s in range(ns):
        ref[pl.ds(s, m, stride=ns), :] = w[:, s * TILE_LANES:(s + 1) * TILE_LANES]


def _unpack_rows(ref, s, m, ns):
    w = ref[pl.ds(s, m, stride=ns), :]
    lo = pltpu.bitcast(w << 16, F32)
    hi = pltpu.bitcast(w & jnp.uint32(0xFFFF0000), F32)
    return lo, hi


def _router_kernel(x_ref, g_ref, mod_ref, rw_ref, rb_ref, hp_ref, idx_ref, gate_ref, pos_ref, cnt_ref,
                   run_ref, *, tm):
    i = pl.program_id(0)

    @pl.when(i == 0)
    def _():
        run_ref[...] = jnp.zeros_like(run_ref)

    h = _modnorm(x_ref[...], g_ref[...], mod_ref[3:4, :], mod_ref[4:5, :])
    _pack_rows(hp_ref, h)
    logits = jnp.dot(h, rw_ref[...], preferred_element_type=F32, precision=lax.Precision.HIGHEST)
    scores = jax.nn.sigmoid(logits)
    n_e = scores.shape[1]
    lane = lax.broadcasted_iota(jnp.int32, scores.shape, 1).astype(F32)
    work = scores + rb_ref[...]
    sel = jnp.zeros_like(scores)
    hot, idx, gates = [], [], []
    for _ in range(TOP_K):
        m = jnp.max(work, axis=-1, keepdims=True)
        first = jnp.min(jnp.where(work == m, lane, float(n_e)), axis=-1, keepdims=True)
        one = lane == first
        work = jnp.where(one, -jnp.inf, work)
        hot.append(one)
        idx.append(first)
        gates.append(jnp.sum(jnp.where(one, scores, 0.0), axis=-1, keepdims=True))
        sel = sel + one.astype(F32)
    total = gates[0]
    for gj in gates[1:]:
        total = total + gj
    t_i = lax.broadcasted_iota(jnp.int32, (tm, tm), 0)
    s_i = lax.broadcasted_iota(jnp.int32, (tm, tm), 1)
    before = _dot((s_i < t_i).astype(BF16), sel.astype(BF16)) + run_ref[...]
    run_ref[...] = run_ref[...] + jnp.sum(sel, axis=0, keepdims=True)
    cnt_ref[...] = run_ref[...]
    slot = lax.broadcasted_iota(jnp.int32, (tm, SLOT), 1)
    idx_o = jnp.zeros((tm, SLOT), jnp.int32)
    gate_o = jnp.zeros((tm, SLOT), F32)
    pos_o = jnp.zeros((tm, SLOT), jnp.int32)
    for j in range(TOP_K):
        pos_j = jnp.sum(jnp.where(hot[j], before, 0.0), axis=-1, keepdims=True).astype(jnp.int32)
        idx_o = jnp.where(slot == j, idx[j].astype(jnp.int32), idx_o)
        gate_o = jnp.where(slot == j, gates[j] / total * ROUTED_SCALE, gate_o)
        pos_o = jnp.where(slot == j, pos_j, pos_o)
    idx_ref[...] = idx_o
    gate_ref[...] = gate_o
    pos_ref[...] = pos_o


def router(x, g, mod3, router_w, router_bias, seqs, *, tm=256):
    t, d = x.shape
    n_e = router_w.shape[1]
    ns = d // (2 * TILE_LANES)
    tok = lambda i: (i, 0)
    return pl.pallas_call(
        functools.partial(_router_kernel, tm=tm),
        grid=(t // tm,),
        in_specs=[pl.BlockSpec((tm, d), tok),
                  pl.BlockSpec((1, d), lambda i: (0, 0)),
                  pl.BlockSpec((None, 6, d), lambda i: (seqs.seq_of_tile(i, tm), 0, 0)),
                  pl.BlockSpec((d, n_e), lambda i: (0, 0)),
                  pl.BlockSpec((1, n_e), lambda i: (0, 0))],
        out_specs=[pl.BlockSpec((tm * ns, TILE_LANES), tok),
                   pl.BlockSpec((tm, SLOT), tok), pl.BlockSpec((tm, SLOT), tok), pl.BlockSpec((tm, SLOT), tok),
                   pl.BlockSpec((1, n_e), lambda i: (0, 0))],
        out_shape=[jax.ShapeDtypeStruct((t * ns, TILE_LANES), jnp.uint32),
                   jax.ShapeDtypeStruct((t, SLOT), jnp.int32),
                   jax.ShapeDtypeStruct((t, SLOT), F32),
                   jax.ShapeDtypeStruct((t, SLOT), jnp.int32),
                   jax.ShapeDtypeStruct((1, n_e), F32)],
        scratch_shapes=[pltpu.VMEM((1, n_e), F32)],
        name="moe_router",
        compiler_params=_params("arbitrary"),
    )(x, g, mod3, router_w, router_bias)


def _ffn_rows(xs_ref, wg_ref, wu_ref, wd_ref, xb_ref):
    bm = xb_ref.shape[0]
    half = xb_ref.shape[1] // 2
    ns = half // TILE_LANES
    for s in range(ns):
        lo, hi = _unpack_rows(xs_ref, s, bm, ns)
        xb_ref[:, s * TILE_LANES:(s + 1) * TILE_LANES] = lo.astype(BF16)
        xb_ref[:, half + s * TILE_LANES:half + (s + 1) * TILE_LANES] = hi.astype(BF16)
    xb = xb_ref[...]
    hg = _dot(xb, wg_ref[...])
    hu = _dot(xb, wu_ref[...])
    hb = (hg * jax.nn.sigmoid(hg) * hu).astype(BF16)
    return _dot(hb, wd_ref[...])


def _shared_dispatch_kernel(dest_ref, zblk_ref, hp_ref, wg_ref, wu_ref, wd_ref, o_ref, xs_ref, xb_ref, zbuf_ref,
                            sem, zsem, *, tm, ns, bm):
    @pl.when(pl.program_id(0) == 0)
    def _():
        zbuf_ref[...] = jnp.zeros_like(zbuf_ref)

        def zero_copy(k):
            row0 = pl.multiple_of(zblk_ref[k] * (bm * ns), bm * ns)
            return pltpu.make_async_copy(zbuf_ref, xs_ref.at[pl.ds(row0, bm * ns)], zsem)

        def zero_issue(k, carry):
            @pl.when(zblk_ref[k] >= 0)
            def _():
                zero_copy(k).start()
            return carry

        def zero_drain(k, carry):
            @pl.when(zblk_ref[k] >= 0)
            def _():
                zero_copy(k).wait()
            return carry

        lax.fori_loop(0, zblk_ref.shape[0], zero_issue, 0)
        lax.fori_loop(0, zblk_ref.shape[0], zero_drain, 0)

    def row_copy(t, j):
        src = hp_ref.at[pl.ds(pl.multiple_of(t * ns, ns), ns)]
        dst = xs_ref.at[pl.ds(pl.multiple_of(dest_ref[t * SLOT + j], ns), ns)]
        return pltpu.make_async_copy(src, dst, sem)

    def issue(t, carry):
        for j in range(TOP_K):
            row_copy(t, j).start(priority=j % 2)
        return carry

    def drain(t, carry):
        for j in range(TOP_K):
            row_copy(t, j).wait()
        return carry

    lax.fori_loop(0, tm, issue, 0, unroll=DMA_LOOP_UNROLL)
    o_ref[...] = _ffn_rows(hp_ref, wg_ref, wu_ref, wd_ref, xb_ref)
    lax.fori_loop(0, tm, drain, 0, unroll=DMA_LOOP_UNROLL)


def shared_ffn_dispatch(hp, dest_rows, zero_blocks, n_rows, w_a, w_b, w_down, *, bm, tm=256):
    f, d = w_down.shape
    ns = d // (2 * TILE_LANES)
    t = hp.shape[0] // ns
    const = lambda i: (0, 0)
    return pl.pallas_call(
        functools.partial(_shared_dispatch_kernel, tm=tm, ns=ns, bm=bm),
        grid=(t // tm,),
        in_specs=[pl.BlockSpec((tm * SLOT,), lambda i: (i,), memory_space=pltpu.SMEM),
                  pl.BlockSpec(memory_space=pltpu.SMEM),
                  pl.BlockSpec((tm * ns, TILE_LANES), lambda i: (i, 0)),
                  pl.BlockSpec(w_a.shape, const), pl.BlockSpec(w_b.shape, const), pl.BlockSpec((f, d), const)],
        out_specs=[pl.BlockSpec((tm, d), lambda i: (i, 0)), pl.BlockSpec(memory_space=pl.ANY)],
        out_shape=[jax.ShapeDtypeStruct((t, d), F32),
                   jax.ShapeDtypeStruct((n_rows * ns, TILE_LANES), jnp.uint32)],
        scratch_shapes=[pltpu.VMEM((tm, d), BF16), pltpu.VMEM((bm * ns, TILE_LANES), jnp.uint32),
                        pltpu.SemaphoreType.DMA(()), pltpu.SemaphoreType.DMA(())],
        name="moe_shared_expert_dispatch",
        compiler_params=_params("arbitrary"),
    )(dest_rows, zero_blocks, hp, w_a, w_b, w_down)


def _expert_kernel(be_ref, nblk_ref, xs_ref, wg_ref, wu_ref, wd_ref, o_ref, xb_ref):
    i = pl.program_id(0)

    @pl.when(i < nblk_ref[0])
    def _():
        _pack_rows(o_ref, _ffn_rows(xs_ref, wg_ref, wu_ref, wd_ref, xb_ref))

    @pl.when(i >= nblk_ref[0])
    def _():
        o_ref[...] = jnp.zeros_like(o_ref)


def expert_ffn(xs, blk_expert, n_used, w_a, w_b, w_down, *, bm):
    f, d = w_down.shape[1:]
    ns = d // (2 * TILE_LANES)
    n_rows = xs.shape[0] // ns
    rows = pl.BlockSpec((bm * ns, TILE_LANES), lambda i, be, nb: (i, 0))
    return pl.pallas_call(
        _expert_kernel,
        grid_spec=pltpu.PrefetchScalarGridSpec(
            num_scalar_prefetch=2,
            grid=(n_rows // bm,),
            in_specs=[rows,
                      pl.BlockSpec((None,) + w_a.shape[1:], lambda i, be, nb: (be[i], 0, 0)),
                      pl.BlockSpec((None,) + w_b.shape[1:], lambda i, be, nb: (be[i], 0, 0)),
                      pl.BlockSpec((None, f, d), lambda i, be, nb: (be[i], 0, 0))],
            out_specs=rows,
            scratch_shapes=[pltpu.VMEM((bm, d), BF16)]),
        out_shape=jax.ShapeDtypeStruct((n_rows * ns, TILE_LANES), jnp.uint32),
        name="moe_experts",
        compiler_params=_params("arbitrary"),
    )(blk_expert, n_used, xs, w_a, w_b, w_down)


def _combine_kernel(dest_ref, dest_next_ref, ys_ref, gate_ref, sh_ref, x_ref, mod_ref, o_ref, buf_ref, sems,
                    *, tm, ns):
    i = pl.program_id(0)
    n = pl.num_programs(0)
    slot = lax.rem(i, 2)

    def row_copy(dref, t, j, sl):
        src = ys_ref.at[pl.ds(pl.multiple_of(dref[t * SLOT + j], ns), ns)]
        dst = buf_ref.at[sl * TOP_K + j, pl.ds(pl.multiple_of(t * ns, ns), ns)]
        return pltpu.make_async_copy(src, dst, sems.at[sl])

    def gather(dref, sl):
        def issue(t, carry):
            for j in range(TOP_K):
                row_copy(dref, t, j, sl).start()
            return carry
        lax.fori_loop(0, tm, issue, 0, unroll=DMA_LOOP_UNROLL)

    @pl.when(i == 0)
    def _():
        gather(dest_ref, 0)

    @pl.when(i + 1 < n)
    def _():
        gather(dest_next_ref, 1 - slot)

    def drain(t, carry):
        for j in range(TOP_K):
            row_copy(dest_ref, t, j, slot).wait()
        return carry

    lax.fori_loop(0, tm, drain, 0, unroll=DMA_LOOP_UNROLL)
    gates = gate_ref[...]
    gj = [jnp.broadcast_to(gates[:, j:j + 1], (tm, TILE_LANES)) for j in range(TOP_K)]
    gate_f = mod_ref[5:6, :]
    half = x_ref.shape[1] // 2
    for s in range(ns):
        acc_lo = jnp.zeros((tm, TILE_LANES), F32)
        acc_hi = jnp.zeros((tm, TILE_LANES), F32)
        for j in range(TOP_K):
            lo, hi = _unpack_rows(buf_ref.at[slot * TOP_K + j], s, tm, ns)
            acc_lo = acc_lo + gj[j] * lo
            acc_hi = acc_hi + gj[j] * hi
        for acc, off in ((acc_lo, s * TILE_LANES), (acc_hi, half + s * TILE_LANES)):
            sl = slice(off, off + TILE_LANES)
            o_ref[:, sl] = x_ref[:, sl] + gate_f[:, sl] * (acc + sh_ref[:, sl])


def combine(ys, dest_rows, gates, shared, x, mod3, seqs, *, tm=128):
    t, d = x.shape
    ns = d // (2 * TILE_LANES)
    n = t // tm
    tok = lambda i: (i, 0)
    return pl.pallas_call(
        functools.partial(_combine_kernel, tm=tm, ns=ns),
        grid=(n,),
        in_specs=[pl.BlockSpec((tm * SLOT,), lambda i: (i,), memory_space=pltpu.SMEM),
                  pl.BlockSpec((tm * SLOT,), lambda i: (jnp.minimum(i + 1, n - 1),), memory_space=pltpu.SMEM),
                  pl.BlockSpec(memory_space=pl.ANY),
                  pl.BlockSpec((tm, SLOT), tok),
                  pl.BlockSpec((tm, d), tok),
                  pl.BlockSpec((tm, d), tok),
                  pl.BlockSpec((None, 6, d), lambda i: (seqs.seq_of_tile(i, tm), 0, 0))],
        out_specs=pl.BlockSpec((tm, d), tok),
        out_shape=jax.ShapeDtypeStruct((t, d), F32),
        scratch_shapes=[pltpu.VMEM((2 * TOP_K, tm * ns, TILE_LANES), jnp.uint32),
                        pltpu.SemaphoreType.DMA((2,))],
        name="moe_combine",
        compiler_params=_params("arbitrary"),
    )(dest_rows, dest_rows, ys, gates, shared, x, mod3)


def moe_block(x, mod3, g, router_w, router_bias, w_gate, w_up, w_down, sh_gate, sh_up, sh_down, layer, seqs,
              *, bm=256):
    t, d = x.shape
    n_e = router_w.shape[1]
    hp, idx, gates, pos, counts = router(x, g, mod3, router_w, router_bias, seqs)
    cnt = counts[0].astype(jnp.int32)
    padded = (cnt + bm - 1) // bm * bm
    end_pad = jnp.cumsum(padded)
    start_pad = end_pad - padded
    ns = d // (2 * TILE_LANES)
    dest_rows = ((jnp.take(start_pad, idx, axis=0) + pos) * ns).reshape(-1)
    nblk = -(-(t * TOP_K) // bm) + n_e
    blk_start = jnp.arange(nblk, dtype=jnp.int32) * bm
    blk_expert = jnp.minimum(jnp.sum(end_pad[None, :] <= blk_start[:, None], axis=1), n_e - 1).astype(jnp.int32)
    n_used = (end_pad[-1:] // bm).astype(jnp.int32)
    last_blk = jnp.where(padded > 0, end_pad // bm - 1, -1)
    tail_blk = n_used[0] + jnp.arange(n_e, dtype=jnp.int32)
    zero_blocks = jnp.concatenate([last_blk, jnp.where(tail_blk < nblk, tail_blk, -1)]).astype(jnp.int32)
    sh = lambda w: cast_bf16(w[:, None], layer)[0]
    shared, xs = shared_ffn_dispatch(hp, dest_rows, zero_blocks, nblk * bm, sh(sh_gate), sh(sh_up), sh(sh_down),
                                     bm=bm)
    bf = lambda w: cast_bf16(w, layer)
    ys = expert_ffn(xs, blk_expert, n_used, bf(w_gate), bf(w_up), bf(w_down), bm=bm)
    return combine(ys, dest_rows, gates, shared, x, mod3, seqs)


def _final_norm_kernel(x_ref, g_ref, o_ref):
    x = x_ref[...]
    o_ref[...] = x * lax.rsqrt(jnp.mean(x * x, axis=-1, keepdims=True) + RMS_EPS) * g_ref[...]


def final_rmsnorm(x, g, row0, rows, *, tm=256):
    d = x.shape[1]
    b0 = row0 // tm
    return pl.pallas_call(
        _final_norm_kernel,
        grid=(rows // tm,),
        in_specs=[pl.BlockSpec((tm, d), lambda i: (b0 + i, 0)), pl.BlockSpec((1, d), lambda i: (0, 0))],
        out_specs=pl.BlockSpec((tm, d), lambda i: (i, 0)),
        out_shape=jax.ShapeDtypeStruct((rows, d), F32),
        name="final_norm",
        compiler_params=_params("parallel"),
    )(x, g)


def _block_diag2(w):
    k, n = w.shape[1:]
    z = jnp.zeros((k, n), w.dtype)
    return jnp.concatenate([jnp.concatenate([w[0], z], axis=1), jnp.concatenate([z, w[1]], axis=1)], axis=0)


def rwkv_block(x, mod3, g, seqs, mu, w_r, w_k, w_v, w_o, dec_w0, dec_w1, dec_w2, iclr_a0, iclr_a1, iclr_a2,
               gate_g1, gate_g2, k_k, k_a, r_k, gn_w, gn_b):
    d = x.shape[1]
    bf = lambda w: w.astype(BF16)
    xr, xw, xk, xv, xa, xg = shiftmix(x, g, mod3, mu, seqs)
    r = mm(xr, bf(w_r), out_dtype=F32, name="rwkv_proj")
    k = mm(xk, bf(w_k), out_dtype=F32, name="rwkv_proj")
    v = mm(xv, bf(w_v), out_dtype=F32, name="rwkv_proj")
    lw_in = mm(xw, bf(jnp.concatenate([dec_w1[0], dec_w1[1]], axis=1)), out_dtype=BF16, epi=_epi_tanh,
               name="rwkv_decay_lora_in")
    la_in = mm(xa, bf(jnp.concatenate([iclr_a1[0], iclr_a1[1]], axis=1)), out_dtype=BF16, name="rwkv_iclr_lora_in")
    lg_in = mm(xg, bf(gate_g1), out_dtype=BF16, epi=_epi_sigmoid, name="rwkv_gate_lora_in")
    gate = mm(lg_in, bf(gate_g2), out_dtype=F32, name="rwkv_gate_lora_out")
    lw2 = mm(lw_in, bf(_block_diag2(dec_w2)), out_dtype=F32, epi=_epi_log_decay,
             extras=[(dec_w0.reshape(1, 2 * d), "col")], name="rwkv_log_decay")
    a2 = mm(la_in, bf(_block_diag2(iclr_a2)), out_dtype=F32, epi=_epi_bias_sigmoid,
            extras=[(iclr_a0.reshape(1, 2 * d), "col")], name="rwkv_iclr")
    k_a2 = k_a.reshape(1, d)
    yf, yb = wkv(r, k, v, lw2, a2, k_k.reshape(1, d), k_a2, seqs)
    yo = rwkv_post(yf, yb, r, k, v, a2, gate, k_a2, r_k.reshape(1, d), gn_w.reshape(1, d), gn_b.reshape(1, d))
    return mm(yo, bf(w_o), out_dtype=F32, epi=_epi_residual(2), extras=[(x, "row"), (mod3, "seq")], seqs=seqs,
              name="mixer_out_proj")


def attention_block(x, mod3, g, seqs, w_qkv, w_o, sink):
    d = x.shape[1]
    h = modnorm(x, g, mod3, seqs, row=0)
    qkv = mm(h, w_qkv.astype(BF16), out_dtype=BF16, name="attn_qkv")
    o = window_attention(qkv, sink, seqs, d=d)
    return mm(o, w_o.astype(BF16), out_dtype=F32, epi=_epi_residual(2), extras=[(x, "row"), (mod3, "seq")],
              seqs=seqs, name="mixer_out_proj")


def kernel(x_prompt, x_sample, c_prompt, c_sample, ada_w, ada_b, norm_mix, norm_ffn, rw_mu, rw_wr, rw_wk, rw_wv, rw_wo, rw_dec_w0, rw_dec_w1, rw_dec_w2, rw_iclr_a0, rw_iclr_a1, rw_iclr_a2, rw_gate_g1, rw_gate_g2, rw_k_k, rw_k_a, rw_r_k, rw_gn_w, rw_gn_b, at_w_qkv, at_w_o, at_sink, router_w, router_bias, moe_w_gate, moe_w_up, moe_w_down, sh_w_gate, sh_w_up, sh_w_down, final_norm):
    b1, s1, d = x_prompt.shape
    b2, s2, _ = x_sample.shape
    seqs = Seqs(b1, s1, b2, s2)
    depth = ada_w.shape[0]
    x = jnp.concatenate([x_prompt.reshape(b1 * s1, d), x_sample.reshape(b2 * s2, d)], axis=0)
    c = jnp.concatenate([c_prompt, c_sample], axis=0)
    rows = -(-seqs.nseq // TILE_SUBLANES) * TILE_SUBLANES
    c_act = jnp.pad(jax.nn.silu(c), ((0, rows - seqs.nseq), (0, 0)))
    for i in range(depth):
        j = i // 2
        mod = mm(c_act, ada_w[i], out_dtype=F32, epi=_epi_bias, extras=[(ada_b[i].reshape(1, -1), "col")],
                 nb=d, precision=lax.Precision.HIGHEST, name="adaln_mod")
        mod3 = mod[:seqs.nseq].reshape(seqs.nseq, 6, d)
        g_mix = norm_mix[i].reshape(1, d)
        if i % 2 == 0:
            x = rwkv_block(x, mod3, g_mix, seqs, rw_mu[j], rw_wr[j], rw_wk[j], rw_wv[j], rw_wo[j], rw_dec_w0[j],
                           rw_dec_w1[j], rw_dec_w2[j], rw_iclr_a0[j], rw_iclr_a1[j], rw_iclr_a2[j],
                           rw_gate_g1[j], rw_gate_g2[j], rw_k_k[j], rw_k_a[j], rw_r_k[j], rw_gn_w[j], rw_gn_b[j])
        else:
            x = attention_block(x, mod3, g_mix, seqs, at_w_qkv[j], at_w_o[j], at_sink[j])
        x = moe_block(x, mod3, norm_ffn[i].reshape(1, d), router_w[i], router_bias[i].reshape(1, -1),
                      moe_w_gate, moe_w_up, moe_w_down, sh_w_gate, sh_w_up, sh_w_down, i, seqs)
    fw = final_norm.reshape(1, d)
    y1 = final_rmsnorm(x, fw, 0, seqs.t1).reshape(b1, s1, d)
    y2 = final_rmsnorm(x, fw, seqs.t1, b2 * s2).reshape(b2, s2, d)
    return (y1, y2)
```

```python
import functools
from typing import NamedTuple

import jax
import jax.numpy as jnp
import numpy as np
from jax import lax
from jax.experimental import pallas as pl
from jax.experimental.pallas import tpu as pltpu

F32 = jnp.float32
BF16 = jnp.bfloat16

HEAD = 64
GROUP_LANES = 256
CHUNK = 64
ATT_GROUP = 4
ATT_KV_BATCH = 1
WINDOW = 128
ATT_BLOCK = 128
TOP_K = 6
ROUTED_SCALE = 2.446
RMS_EPS = 1e-6
GN_EPS = 64e-5
NEG_INF = -1e30
SLOT = 8
DMA_LOOP_UNROLL = 8
TILE_SUBLANES = 8
TILE_LANES = 128
VMEM_LIMIT = 56 * 1024 * 1024


class Seqs(NamedTuple):
    b1: int
    s1: int
    b2: int
    s2: int

    @property
    def t1(self):
        return self.b1 * self.s1

    @property
    def total(self):
        return self.b1 * self.s1 + self.b2 * self.s2

    @property
    def nseq(self):
        return self.b1 + self.b2

    def seq_of_tile(self, i, tm):
        n1 = self.t1 // tm
        return jnp.where(i < n1, i // (self.s1 // tm), self.b1 + (i - n1) // (self.s2 // tm))

    def starts_seq(self, tok):
        return jnp.where(tok < self.t1, lax.rem(tok, self.s1) == 0, lax.rem(tok - self.t1, self.s2) == 0)

    def ends_seq(self, tok_end):
        return jnp.where(tok_end <= self.t1, lax.rem(tok_end, self.s1) == 0,
                         lax.rem(tok_end - self.t1, self.s2) == 0)


def _params(*sem):
    return pltpu.CompilerParams(dimension_semantics=sem, vmem_limit_bytes=VMEM_LIMIT)


def _modnorm(x, g, shift, scale):
    y = x * lax.rsqrt(jnp.mean(x * x, axis=-1, keepdims=True) + RMS_EPS)
    return (y * g) * (1.0 + scale) + shift


def _split_bf16(x):
    hi = x.astype(BF16)
    lo = (x - hi.astype(F32)).astype(BF16)
    return hi, lo


def _dot(a, b):
    return jnp.dot(a, b, preferred_element_type=F32)


def _dot_nt(a, b):
    return lax.dot_general(a, b, (((1,), (1,)), ((), ())), preferred_element_type=F32)


def _dot_tn(a, b):
    return lax.dot_general(a, b, (((0,), (0,)), ((), ())), preferred_element_type=F32)


def _block_mask():
    r = lax.broadcasted_iota(jnp.int32, (GROUP_LANES, GROUP_LANES), 0) // HEAD
    c = lax.broadcasted_iota(jnp.int32, (GROUP_LANES, GROUP_LANES), 1) // HEAD
    return r == c


def _group_sum(x, ones_bd):
    hi, lo = _split_bf16(x)
    return _dot(hi, ones_bd) + _dot(lo, ones_bd)


def _modnorm_kernel(x_ref, g_ref, mod_ref, o_ref, *, row):
    h = _modnorm(x_ref[...], g_ref[...], mod_ref[row:row + 1, :], mod_ref[row + 1:row + 2, :])
    o_ref[...] = h.astype(o_ref.dtype)


def modnorm(x, g, mod3, seqs, *, row, tm=256):
    t, d = x.shape
    return pl.pallas_call(
        functools.partial(_modnorm_kernel, row=row),
        grid=(t // tm,),
        in_specs=[pl.BlockSpec((tm, d), lambda i: (i, 0)),
                  pl.BlockSpec((1, d), lambda i: (0, 0)),
                  pl.BlockSpec((None, 6, d), lambda i: (seqs.seq_of_tile(i, tm), 0, 0))],
        out_specs=pl.BlockSpec((tm, d), lambda i: (i, 0)),
        out_shape=jax.ShapeDtypeStruct((t, d), BF16),
        name="modnorm",
        compiler_params=_params("parallel"),
    )(x, g, mod3)


def _shiftmix_kernel(x_ref, xp_ref, xn_ref, g_ref, mod_ref, mu_ref, *o_refs, seqs, tm):
    i = pl.program_id(0)
    d = x_ref.shape[1]
    first = seqs.starts_seq(i * tm)
    last = seqs.ends_seq((i + 1) * tm)

    def inv_rms(ref):
        x = ref[...]
        return lax.rsqrt(jnp.mean(x * x, axis=-1, keepdims=True) + RMS_EPS)

    inv, inv_p, inv_n = inv_rms(x_ref), inv_rms(xp_ref), inv_rms(xn_ref)
    rows = lax.broadcasted_iota(jnp.int32, (tm, GROUP_LANES), 0)
    for c in range(d // GROUP_LANES):
        sl = slice(c * GROUP_LANES, (c + 1) * GROUP_LANES)
        mod = lambda x, r: (x * r * g_ref[:, sl]) * (1.0 + mod_ref[1:2, sl]) + mod_ref[0:1, sl]
        h = mod(x_ref[:, sl], inv)
        hp = mod(xp_ref[:, sl], inv_p)[TILE_SUBLANES - 1:TILE_SUBLANES, :]
        hn = mod(xn_ref[:, sl], inv_n)[0:1, :]
        hp = jnp.where(first, 0.0, hp)
        hn = jnp.where(last, 0.0, hn)
        prev = jnp.where(rows == 0, hp, pltpu.roll(h, 1, 0))
        nxt = jnp.where(rows == tm - 1, hn, pltpu.roll(h, tm - 1, 0))
        xx = 0.5 * (prev + nxt) - h
        for j, o_ref in enumerate(o_refs):
            o_ref[:, sl] = (h + xx * mu_ref[j:j + 1, sl]).astype(o_ref.dtype)


def shiftmix(x, g, mod3, mu, seqs, *, tm=256):
    t, d = x.shape
    nb8 = t // TILE_SUBLANES
    r8 = tm // TILE_SUBLANES
    return pl.pallas_call(
        functools.partial(_shiftmix_kernel, seqs=seqs, tm=tm),
        grid=(t // tm,),
        in_specs=[pl.BlockSpec((tm, d), lambda i: (i, 0)),
                  pl.BlockSpec((TILE_SUBLANES, d), lambda i: (jnp.maximum(i * r8 - 1, 0), 0)),
                  pl.BlockSpec((TILE_SUBLANES, d), lambda i: (jnp.minimum((i + 1) * r8, nb8 - 1), 0)),
                  pl.BlockSpec((1, d), lambda i: (0, 0)),
                  pl.BlockSpec((None, 6, d), lambda i: (seqs.seq_of_tile(i, tm), 0, 0)),
                  pl.BlockSpec((6, d), lambda i: (0, 0))],
        out_specs=[pl.BlockSpec((tm, d), lambda i: (i, 0))] * 6,
        out_shape=[jax.ShapeDtypeStruct((t, d), BF16)] * 6,
        name="rwkv_shiftmix",
        compiler_params=_params("parallel"),
    )(x, x, x, g, mod3, mu)


def _cast_kernel(w_ref, o_ref):
    o_ref[...] = w_ref[...].astype(o_ref.dtype)


def cast_bf16(w, layer):
    _, e, k, n = w.shape
    return pl.pallas_call(
        _cast_kernel,
        grid=(e,),
        in_specs=[pl.BlockSpec((None, None, k, n), lambda i: (layer, i, 0, 0))],
        out_specs=pl.BlockSpec((None, k, n), lambda i: (i, 0, 0)),
        out_shape=jax.ShapeDtypeStruct((e, k, n), BF16),
        name="cast_bf16",
        compiler_params=_params("parallel"),
    )(w)


def _mm_kernel(a_ref, w_ref, *refs, epi, tn, n_extra, precision):
    extra = refs[:n_extra]
    o_ref = refs[n_extra]
    n = o_ref.shape[-1]
    a = a_ref[...]
    for c in range(n // tn):
        sl = slice(c * tn, (c + 1) * tn)
        acc = jnp.dot(a, w_ref[:, sl], preferred_element_type=F32, precision=precision)
        if epi is not None:
            acc = epi(acc, *[e[:, sl] for e in extra])
        o_ref[:, sl] = acc.astype(o_ref.dtype)


def mm(a, w, *, out_dtype, name, epi=None, extras=(), seqs=None, tm=512, tn=512, nb=None, precision=None):
    m, k = a.shape
    n = w.shape[1]
    nb = n if nb is None else nb
    tn = max([c for c in range(TILE_LANES, min(tn, nb) + 1, TILE_LANES) if nb % c == 0], default=nb)
    tm = min(tm, m)
    specs = [pl.BlockSpec((tm, k), lambda i, j: (i, 0)), pl.BlockSpec((k, nb), lambda i, j: (0, j))]
    args = [a, w]
    for arr, kind in extras:
        if kind == "col":
            specs.append(pl.BlockSpec((1, nb), lambda i, j: (0, j)))
        elif kind == "row":
            specs.append(pl.BlockSpec((tm, nb), lambda i, j: (i, j)))
        else:
            specs.append(pl.BlockSpec((None, arr.shape[1], nb),
                                      lambda i, j: (seqs.seq_of_tile(i, tm), 0, j)))
        args.append(arr)
    return pl.pallas_call(
        functools.partial(_mm_kernel, epi=epi, tn=tn, n_extra=len(extras), precision=precision),
        grid=(m // tm, n // nb),
        in_specs=specs,
        out_specs=pl.BlockSpec((tm, nb), lambda i, j: (i, j)),
        out_shape=jax.ShapeDtypeStruct((m, n), out_dtype),
        name=name,
        compiler_params=_params("parallel", "arbitrary"),
    )(*args)


def _epi_bias(acc, b):
    return acc + b


def _epi_tanh(acc):
    return jnp.tanh(acc)


def _epi_sigmoid(acc):
    return jax.nn.sigmoid(acc)


def _sigmoid(x):
    return 0.5 * jnp.tanh(0.5 * x) + 0.5


def _epi_log_decay(acc, w0):
    return -float(np.exp(-0.5)) * _sigmoid(acc + w0)


def _epi_bias_sigmoid(acc, b):
    return _sigmoid(acc + b)


def _epi_residual(row):
    def epi(acc, x, mod):
        return x + mod[row:row + 1, :] * acc
    return epi


def _expand_bd(m, bm):
    mb = m.astype(BF16)
    return jnp.where(bm, jnp.concatenate([mb] * (GROUP_LANES // HEAD), axis=0), jnp.zeros((), BF16))


def _wkv_chunks(chains, bm, ones_bd):
    c = CHUNK
    n = len(chains)
    rev = [ch[8] for ch in chains]
    t_i = lax.broadcasted_iota(jnp.int32, (c, c), 0)
    s_i = lax.broadcasted_iota(jnp.int32, (c, c), 1)
    tri = {False: (s_i <= t_i).astype(BF16), True: (s_i >= t_i).astype(BF16)}
    t2 = lax.broadcasted_iota(jnp.int32, (c, GROUP_LANES), 0)
    s2 = lax.broadcasted_iota(jnp.int32, (c, GROUP_LANES), 1) % c
    strict = {False: s2 < t2, True: s2 > t2}
    incl = {False: s2 <= t2, True: s2 >= t2}
    eye = jnp.where(s2 == t2, 1.0, 0.0)

    k2 = [ch[1] * ch[5] for ch in chains]
    sq = [piece for q in k2 for piece in _split_bf16(q * q)]
    ss_all = _dot(jnp.concatenate(sq, axis=0), ones_bd)
    ss = [ss_all[2 * j * c:(2 * j + 1) * c] + ss_all[(2 * j + 1) * c:(2 * j + 2) * c] for j in range(n)]
    kk = [q / jnp.maximum(jnp.sqrt(s), 1e-12) for q, s in zip(k2, ss)]
    kd = [ch[1] * (1.0 + (ch[4] - 1.0) * ch[6]) for ch in chains]
    cw = [None] * n
    for rv in (False, True):
        ids = [j for j in range(n) if rev[j] == rv]
        if ids:
            lw = jnp.concatenate([piece for j in ids for piece in _split_bf16(chains[j][3])], axis=1)
            cw_all = _dot(tri[rv], lw)
            for pos, j in enumerate(ids):
                base = 2 * pos * GROUP_LANES
                cw[j] = (cw_all[:, base:base + GROUP_LANES]
                         + cw_all[:, base + GROUP_LANES:base + 2 * GROUP_LANES])
    edge = [w[0:1, :] if rv else w[c - 1:c, :] for w, rv in zip(cw, rev)]
    e_neg = [jnp.exp(-w) for w in cw]
    al = [-q * jnp.exp(w - ch[3]) for q, w, ch in zip(kk, cw, chains)]
    rb = [ch[0] * jnp.exp(w) for w, ch in zip(cw, chains)]
    ka = [q * ch[4] for q, ch in zip(kk, chains)]
    be = [q * e for q, e in zip(ka, e_neg)]
    kb = [q * e for q, e in zip(kd, e_neg)]
    x = [jnp.concatenate([p, q], axis=0).astype(BF16) for p, q in zip(al, rb)]
    z_bd = [jnp.concatenate([_expand_bd(p, bm), _expand_bd(q, bm)], axis=0) for p, q in zip(be, kb)]
    a_all = [_dot_nt(p, q) for p, q in zip(x, z_bd)]
    a_ab = [jnp.where(strict[rv], m[:c, :GROUP_LANES], 0.0) for m, rv in zip(a_all, rev)]
    a_kr = [jnp.concatenate([jnp.where(strict[rv], m[:c, GROUP_LANES:], 0.0),
                             jnp.where(incl[rv], m[c:, GROUP_LANES:], 0.0)], axis=0).astype(BF16)
            for m, rv in zip(a_all, rev)]
    a_rb = [jnp.where(incl[rv], m[c:, :GROUP_LANES], 0.0).astype(BF16) for m, rv in zip(a_all, rev)]
    p = [eye + m for m in a_ab]
    ap = [_dot(m.astype(BF16), _expand_bd(m, bm)) for m in a_ab]
    for _ in range(4):
        both = [_dot(jnp.concatenate([m, q], axis=0).astype(BF16), _expand_bd(m, bm)) for m, q in zip(ap, p)]
        p = [q + b[c:] for q, b in zip(p, both)]
        ap = [b[:c] for b in both]
    p = [q + _dot(q.astype(BF16), _expand_bd(m, bm)) for q, m in zip(p, ap)]
    xh = [_dot_nt(q, ch[7].astype(BF16)) for q, ch in zip(x, chains)]
    av = [_dot(m, _expand_bd(ch[2], bm)) for m, ch in zip(a_kr, chains)]
    u = [_dot(q.astype(BF16), _expand_bd(h[:c] + w[:c], bm)) for q, h, w in zip(p, xh, av)]
    y = [h[c:] + w[c:] + _dot(m, _expand_bd(q, bm)) for h, w, m, q in zip(xh, av, a_rb, u)]
    e_rest = [jnp.exp(e - w) for e, w in zip(edge, cw)]
    zw = [jnp.concatenate([p_ * e, q * e], axis=0).astype(BF16) for p_, q, e in zip(ka, kd, e_rest)]
    uv = [jnp.concatenate([q, ch[2]], axis=0).astype(BF16) for q, ch in zip(u, chains)]
    upd = [_dot_tn(p_, q) for p_, q in zip(uv, zw)]
    ht_new = [chains[j][7] * jnp.exp(edge[j]) + jnp.where(bm, upd[j], 0.0) for j in range(n)]
    return y, ht_new


def _wkv_kernel(rf, kf, vf, lwf, af, rb, kb, vb, lwb, ab, kk_ref, ka_ref, yf_ref, yb_ref, st_ref,
                *, seqs, ct, ngb):
    i = pl.program_id(1)
    nt = pl.num_programs(1)
    nc = ct // CHUNK
    bm = _block_mask()
    ones_bd = bm.astype(BF16)

    def load(refs, lanes, off, st_idx, reset, reverse):
        rows = pl.ds(off, CHUNK)
        r, k, v, lw, a = [ref[rows, lanes] for ref in refs]
        ht = jnp.where(reset, 0.0, st_ref[st_idx])
        return (r, k, v, lw, a, kk_ref[:, lanes], ka_ref[:, lanes], ht, reverse)

    def body(c, carry):
        off_f = pl.multiple_of(c * CHUNK, CHUNK)
        off_b = pl.multiple_of((nc - 1 - c) * CHUNK, CHUNK)
        tok_b = (nt - 1 - i) * ct + off_b
        reset_f = seqs.starts_seq(i * ct + off_f)
        reset_b = seqs.ends_seq(tok_b + CHUNK)
        lanes = [slice(g * GROUP_LANES, (g + 1) * GROUP_LANES) for g in range(ngb)]
        chains = [load((rf, kf, vf, lwf, af), lanes[g], off_f, g, reset_f, False) for g in range(ngb)]
        chains += [load((rb, kb, vb, lwb, ab), lanes[g], off_b, ngb + g, reset_b, True) for g in range(ngb)]
        y, ht = _wkv_chunks(chains, bm, ones_bd)
        for g in range(ngb):
            st_ref[g] = ht[g]
            st_ref[ngb + g] = ht[ngb + g]
            yf_ref[pl.ds(off_f, CHUNK), lanes[g]] = y[g]
            yb_ref[pl.ds(off_b, CHUNK), lanes[g]] = y[ngb + g]
        return carry

    lax.fori_loop(0, nc, body, 0)


def wkv(r, k, v, lw2, a2, k_k, k_a, seqs, *, ct=128, ngb=8):
    t, d = r.shape
    ngb = min(ngb, d // GROUP_LANES)
    wl = ngb * GROUP_LANES
    nlb = d // wl
    nt = t // ct
    fwd = lambda g, i: (i, g)
    bwd = lambda g, i: (nt - 1 - i, g)
    bwd2 = lambda g, i: (nt - 1 - i, nlb + g)
    blk = (ct, wl)
    par = pl.BlockSpec((1, wl), lambda g, i: (0, g))
    return pl.pallas_call(
        functools.partial(_wkv_kernel, seqs=seqs, ct=ct, ngb=ngb),
        grid=(nlb, nt),
        in_specs=[pl.BlockSpec(blk, fwd)] * 5 + [pl.BlockSpec(blk, bwd)] * 3 + [pl.BlockSpec(blk, bwd2)] * 2
                 + [par, par],
        out_specs=[pl.BlockSpec(blk, fwd), pl.BlockSpec(blk, bwd)],
        out_shape=[jax.ShapeDtypeStruct((t, d), F32)] * 2,
        scratch_shapes=[pltpu.VMEM((2 * ngb, GROUP_LANES, GROUP_LANES), F32)],
        name="wkv_scan",
        compiler_params=_params("parallel", "arbitrary"),
    )(r, k, v, lw2, a2, r, k, v, lw2, a2, k_k, k_a)


def _rwkv_post_kernel(yf, yb, r, k, v, af, ab, g, ka, rk, gw, gb, o_ref):
    ones_bd = _block_mask().astype(BF16)
    inv_n = 1.0 / HEAD
    for c in range(o_ref.shape[1] // GROUP_LANES):
        sl = slice(c * GROUP_LANES, (c + 1) * GROUP_LANES)
        y = yf[:, sl] + yb[:, sl]
        mean = _group_sum(y, ones_bd) * inv_n
        yc = y - mean
        var = _group_sum(yc * yc, ones_bd) * inv_n
        yn = yc * lax.rsqrt(var + GN_EPS) * gw[:, sl] + gb[:, sl]
        kv = k[:, sl]
        k_a = ka[:, sl]
        kd_sum = kv * (1.0 + (af[:, sl] - 1.0) * k_a) + kv * (1.0 + (ab[:, sl] - 1.0) * k_a)
        bonus = _group_sum(r[:, sl] * kd_sum * rk[:, sl], ones_bd) * v[:, sl]
        o_ref[:, sl] = ((yn + bonus) * g[:, sl]).astype(o_ref.dtype)


def rwkv_post(yf, yb, r, k, v, a2, g, k_a, r_k, gn_w, gn_b, *, tm=512, lanes=1024):
    t, d = r.shape
    lanes = min(lanes, d)
    ng = d // lanes
    blk = pl.BlockSpec((tm, lanes), lambda i, j: (i, j))
    blk_b = pl.BlockSpec((tm, lanes), lambda i, j: (i, ng + j))
    par = pl.BlockSpec((1, lanes), lambda i, j: (0, j))
    return pl.pallas_call(
        _rwkv_post_kernel,
        grid=(t // tm, ng),
        in_specs=[blk] * 5 + [blk, blk_b, blk] + [par] * 4,
        out_specs=blk,
        out_shape=jax.ShapeDtypeStruct((t, d), BF16),
        name="rwkv_post",
        compiler_params=_params("parallel", "parallel"),
    )(yf, yb, r, k, v, a2, a2, g, k_a, r_k, gn_w, gn_b)


def _attn_kernel(sink_ref, q_ref, kp_ref, kc_ref, kn_ref, vp_ref, vc_ref, vn_ref, o_ref, *, seqs, n_heads):
    i = pl.program_id(0)
    blk = ATT_BLOCK
    span = blk + 2 * WINDOW
    first = seqs.starts_seq(i * blk)
    last = seqs.ends_seq((i + 1) * blk)
    qi = lax.broadcasted_iota(jnp.int32, (blk, span), 0)
    ci = lax.broadcasted_iota(jnp.int32, (blk, span), 1)
    dist_i = jnp.abs(qi - (ci - WINDOW))
    valid = (dist_i <= WINDOW) & ((ci >= WINDOW) | jnp.logical_not(first)) \
        & ((ci < WINDOW + blk) | jnp.logical_not(last))
    dist = dist_i.astype(F32)
    n_kv = n_heads // ATT_GROUP
    kv_batch = min(ATT_KV_BATCH, n_kv)
    outs = []
    for kv0 in range(0, n_kv, kv_batch):
        heads = [(kvh, kvh * ATT_GROUP + g) for kvh in range(kv0, kv0 + kv_batch) for g in range(ATT_GROUP)]
        k_span, v_span = {}, {}
        for kvh in range(kv0, kv0 + kv_batch):
            ks = slice(kvh * HEAD, (kvh + 1) * HEAD)
            k_span[kvh] = jnp.concatenate([kp_ref[:, ks], kc_ref[:, ks], kn_ref[:, ks]], axis=0)
            v_span[kvh] = jnp.concatenate([vp_ref[:, ks], vc_ref[:, ks], vn_ref[:, ks]], axis=0)
        s = [_dot_nt(q_ref[:, h * HEAD:(h + 1) * HEAD], k_span[kvh]) for kvh, h in heads]
        s = [jnp.where(valid, sc * (HEAD ** -0.5) - (2.0 ** (-8.0 * (h + 1) / n_heads)) * dist, NEG_INF)
             for sc, (_, h) in zip(s, heads)]
        m = [jnp.maximum(jnp.max(sc, axis=-1, keepdims=True), sink_ref[h]) for sc, (_, h) in zip(s, heads)]
        e = [jnp.exp(sc - mx) for sc, mx in zip(s, m)]
        denom = [jnp.sum(ex, axis=-1, keepdims=True) + jnp.exp(sink_ref[h] - mx)
                 for ex, mx, (_, h) in zip(e, m, heads)]
        pv = [_dot(ex.astype(BF16), v_span[kvh]) for ex, (kvh, _) in zip(e, heads)]
        outs += [o / dn for o, dn in zip(pv, denom)]
    o_ref[...] = jnp.concatenate(outs, axis=-1).astype(o_ref.dtype)


def window_attention(qkv, sink, seqs, *, d):
    t = qkv.shape[0]
    n_heads = d // HEAD
    kvw = (n_heads // ATT_GROUP) * HEAD
    nb = t // ATT_BLOCK
    kcol = d // kvw
    prev = lambda i: jnp.maximum(i - 1, 0)
    nxt = lambda i: jnp.minimum(i + 1, nb - 1)
    kv_specs = [pl.BlockSpec((ATT_BLOCK, kvw), lambda i, c=c, f=f: (f(i), c))
                for c in (kcol, kcol + 1) for f in (prev, lambda i: i, nxt)]
    return pl.pallas_call(
        functools.partial(_attn_kernel, seqs=seqs, n_heads=n_heads),
        grid=(nb,),
        in_specs=[pl.BlockSpec(memory_space=pltpu.SMEM),
                  pl.BlockSpec((ATT_BLOCK, d), lambda i: (i, 0))] + kv_specs,
        out_specs=pl.BlockSpec((ATT_BLOCK, d), lambda i: (i, 0)),
        out_shape=jax.ShapeDtypeStruct((t, d), BF16),
        name="window_attention",
        compiler_params=_params("parallel"),
    )(sink, qkv, qkv, qkv, qkv, qkv, qkv, qkv)


def _pack_rows(ref, y):
    m = y.shape[0]
    half = y.shape[1] // 2
    ns = half // TILE_LANES
    lo = pltpu.bitcast(y[:, :half].astype(BF16).astype(F32), jnp.uint32) >> 16
    hi = pltpu.bitcast(y[:, half:].astype(BF16).astype(F32), jnp.uint32) & jnp.uint32(0xFFFF0000)
    w = lo | hi
    for s in range(ns):
        ref[pl.ds(s, m, stride=ns), :] = w[:, s * TILE_LANES:(s + 1) * TILE_LANES]


def _unpack_rows(ref, s, m, ns):
    w = ref[pl.ds(s, m, stride=ns), :]
    lo = pltpu.bitcast(w << 16, F32)
    hi = pltpu.bitcast(w & jnp.uint32(0xFFFF0000), F32)
    return lo, hi


def _router_kernel(x_ref, g_ref, mod_ref, rw_ref, rb_ref, hp_ref, idx_ref, gate_ref, pos_ref, cnt_ref,
                   run_ref, *, tm):
    i = pl.program_id(0)

    @pl.when(i == 0)
    def _():
        run_ref[...] = jnp.zeros_like(run_ref)

    h = _modnorm(x_ref[...], g_ref[...], mod_ref[3:4, :], mod_ref[4:5, :])
    _pack_rows(hp_ref, h)
    logits = jnp.dot(h, rw_ref[...], preferred_element_type=F32, precision=lax.Precision.HIGHEST)
    scores = jax.nn.sigmoid(logits)
    n_e = scores.shape[1]
    lane = lax.broadcasted_iota(jnp.int32, scores.shape, 1).astype(F32)
    work = scores + rb_ref[...]
    sel = jnp.zeros_like(scores)
    hot, idx, gates = [], [], []
    for _ in range(TOP_K):
        m = jnp.max(work, axis=-1, keepdims=True)
        first = jnp.min(jnp.where(work == m, lane, float(n_e)), axis=-1, keepdims=True)
        one = lane == first
        work = jnp.where(one, -jnp.inf, work)
        hot.append(one)
        idx.append(first)
        gates.append(jnp.sum(jnp.where(one, scores, 0.0), axis=-1, keepdims=True))
        sel = sel + one.astype(F32)
    total = gates[0]
    for gj in gates[1:]:
        total = total + gj
    t_i = lax.broadcasted_iota(jnp.int32, (tm, tm), 0)
    s_i = lax.broadcasted_iota(jnp.int32, (tm, tm), 1)
    before = _dot((s_i < t_i).astype(BF16), sel.astype(BF16)) + run_ref[...]
    run_ref[...] = run_ref[...] + jnp.sum(sel, axis=0, keepdims=True)
    cnt_ref[...] = run_ref[...]
    slot = lax.broadcasted_iota(jnp.int32, (tm, SLOT), 1)
    idx_o = jnp.zeros((tm, SLOT), jnp.int32)
    gate_o = jnp.zeros((tm, SLOT), F32)
    pos_o = jnp.zeros((tm, SLOT), jnp.int32)
    for j in range(TOP_K):
        pos_j = jnp.sum(jnp.where(hot[j], before, 0.0), axis=-1, keepdims=True).astype(jnp.int32)
        idx_o = jnp.where(slot == j, idx[j].astype(jnp.int32), idx_o)
        gate_o = jnp.where(slot == j, gates[j] / total * ROUTED_SCALE, gate_o)
        pos_o = jnp.where(slot == j, pos_j, pos_o)
    idx_ref[...] = idx_o
    gate_ref[...] = gate_o
    pos_ref[...] = pos_o


def router(x, g, mod3, router_w, router_bias, seqs, *, tm=256):
    t, d = x.shape
    n_e = router_w.shape[1]
    ns = d // (2 * TILE_LANES)
    tok = lambda i: (i, 0)
    return pl.pallas_call(
        functools.partial(_router_kernel, tm=tm),
        grid=(t // tm,),
        in_specs=[pl.BlockSpec((tm, d), tok),
                  pl.BlockSpec((1, d), lambda i: (0, 0)),
                  pl.BlockSpec((None, 6, d), lambda i: (seqs.seq_of_tile(i, tm), 0, 0)),
                  pl.BlockSpec((d, n_e), lambda i: (0, 0)),
                  pl.BlockSpec((1, n_e), lambda i: (0, 0))],
        out_specs=[pl.BlockSpec((tm * ns, TILE_LANES), tok),
                   pl.BlockSpec((tm, SLOT), tok), pl.BlockSpec((tm, SLOT), tok), pl.BlockSpec((tm, SLOT), tok),
                   pl.BlockSpec((1, n_e), lambda i: (0, 0))],
        out_shape=[jax.ShapeDtypeStruct((t * ns, TILE_LANES), jnp.uint32),
                   jax.ShapeDtypeStruct((t, SLOT), jnp.int32),
                   jax.ShapeDtypeStruct((t, SLOT), F32),
                   jax.ShapeDtypeStruct((t, SLOT), jnp.int32),
                   jax.ShapeDtypeStruct((1, n_e), F32)],
        scratch_shapes=[pltpu.VMEM((1, n_e), F32)],
        name="moe_router",
        compiler_params=_params("arbitrary"),
    )(x, g, mod3, router_w, router_bias)


def _ffn_rows(xs_ref, wg_ref, wu_ref, wd_ref, xb_ref):
    bm = xb_ref.shape[0]
    half = xb_ref.shape[1] // 2
    ns = half // TILE_LANES
    for s in range(ns):
        lo, hi = _unpack_rows(xs_ref, s, bm, ns)
        xb_ref[:, s * TILE_LANES:(s + 1) * TILE_LANES] = lo.astype(BF16)
        xb_ref[:, half + s * TILE_LANES:half + (s + 1) * TILE_LANES] = hi.astype(BF16)
    xb = xb_ref[...]
    hg = _dot(xb, wg_ref[...])
    hu = _dot(xb, wu_ref[...])
    hb = (hg * jax.nn.sigmoid(hg) * hu).astype(BF16)
    return _dot(hb, wd_ref[...])


def _shared_dispatch_kernel(dest_ref, zblk_ref, hp_ref, wg_ref, wu_ref, wd_ref, o_ref, xs_ref, xb_ref, zbuf_ref,
                            sem, zsem, *, tm, ns, bm):
    @pl.when(pl.program_id(0) == 0)
    def _():
        zbuf_ref[...] = jnp.zeros_like(zbuf_ref)

        def zero_copy(k):
            row0 = pl.multiple_of(zblk_ref[k] * (bm * ns), bm * ns)
            return pltpu.make_async_copy(zbuf_ref, xs_ref.at[pl.ds(row0, bm * ns)], zsem)

        def zero_issue(k, carry):
            @pl.when(zblk_ref[k] >= 0)
            def _():
                zero_copy(k).start()
            return carry

        def zero_drain(k, carry):
            @pl.when(zblk_ref[k] >= 0)
            def _():
                zero_copy(k).wait()
            return carry

        lax.fori_loop(0, zblk_ref.shape[0], zero_issue, 0)
        lax.fori_loop(0, zblk_ref.shape[0], zero_drain, 0)

    def row_copy(t, j):
        src = hp_ref.at[pl.ds(pl.multiple_of(t * ns, ns), ns)]
        dst = xs_ref.at[pl.ds(pl.multiple_of(dest_ref[t * SLOT + j], ns), ns)]
        return pltpu.make_async_copy(src, dst, sem)

    def issue(t, carry):
        for j in range(TOP_K):
            row_copy(t, j).start(priority=j % 2)
        return carry

    def drain(t, carry):
        for j in range(TOP_K):
            row_copy(t, j).wait()
        return carry

    lax.fori_loop(0, tm, issue, 0, unroll=DMA_LOOP_UNROLL)
    o_ref[...] = _ffn_rows(hp_ref, wg_ref, wu_ref, wd_ref, xb_ref)
    lax.fori_loop(0, tm, drain, 0, unroll=DMA_LOOP_UNROLL)


def shared_ffn_dispatch(hp, dest_rows, zero_blocks, n_rows, w_a, w_b, w_down, *, bm, tm=256):
    f, d = w_down.shape
    ns = d // (2 * TILE_LANES)
    t = hp.shape[0] // ns
    const = lambda i: (0, 0)
    return pl.pallas_call(
        functools.partial(_shared_dispatch_kernel, tm=tm, ns=ns, bm=bm),
        grid=(t // tm,),
        in_specs=[pl.BlockSpec((tm * SLOT,), lambda i: (i,), memory_space=pltpu.SMEM),
                  pl.BlockSpec(memory_space=pltpu.SMEM),
                  pl.BlockSpec((tm * ns, TILE_LANES), lambda i: (i, 0)),
                  pl.BlockSpec(w_a.shape, const), pl.BlockSpec(w_b.shape, const), pl.BlockSpec((f, d), const)],
        out_specs=[pl.BlockSpec((tm, d), lambda i: (i, 0)), pl.BlockSpec(memory_space=pl.ANY)],
        out_shape=[jax.ShapeDtypeStruct((t, d), F32),
                   jax.ShapeDtypeStruct((n_rows * ns, TILE_LANES), jnp.uint32)],
        scratch_shapes=[pltpu.VMEM((tm, d), BF16), pltpu.VMEM((bm * ns, TILE_LANES), jnp.uint32),
                        pltpu.SemaphoreType.DMA(()), pltpu.SemaphoreType.DMA(())],
        name="moe_shared_expert_dispatch",
        compiler_params=_params("arbitrary"),
    )(dest_rows, zero_blocks, hp, w_a, w_b, w_down)


def _expert_kernel(be_ref, nblk_ref, xs_ref, wg_ref, wu_ref, wd_ref, o_ref, xb_ref):
    i = pl.program_id(0)

    @pl.when(i < nblk_ref[0])
    def _():
        _pack_rows(o_ref, _ffn_rows(xs_ref, wg_ref, wu_ref, wd_ref, xb_ref))

    @pl.when(i >= nblk_ref[0])
    def _():
        o_ref[...] = jnp.zeros_like(o_ref)


def expert_ffn(xs, blk_expert, n_used, w_a, w_b, w_down, *, bm):
    f, d = w_down.shape[1:]
    ns = d // (2 * TILE_LANES)
    n_rows = xs.shape[0] // ns
    rows = pl.BlockSpec((bm * ns, TILE_LANES), lambda i, be, nb: (i, 0))
    return pl.pallas_call(
        _expert_kernel,
        grid_spec=pltpu.PrefetchScalarGridSpec(
            num_scalar_prefetch=2,
            grid=(n_rows // bm,),
            in_specs=[rows,
                      pl.BlockSpec((None,) + w_a.shape[1:], lambda i, be, nb: (be[i], 0, 0)),
                      pl.BlockSpec((None,) + w_b.shape[1:], lambda i, be, nb: (be[i], 0, 0)),
                      pl.BlockSpec((None, f, d), lambda i, be, nb: (be[i], 0, 0))],
            out_specs=rows,
            scratch_shapes=[pltpu.VMEM((bm, d), BF16)]),
        out_shape=jax.ShapeDtypeStruct((n_rows * ns, TILE_LANES), jnp.uint32),
        name="moe_experts",
        compiler_params=_params("arbitrary"),
    )(blk_expert, n_used, xs, w_a, w_b, w_down)


def _combine_kernel(dest_ref, dest_next_ref, ys_ref, gate_ref, sh_ref, x_ref, mod_ref, o_ref, buf_ref, sems,
                    *, tm, ns):
    i = pl.program_id(0)
    n = pl.num_programs(0)
    slot = lax.rem(i, 2)

    def row_copy(dref, t, j, sl):
        src = ys_ref.at[pl.ds(pl.multiple_of(dref[t * SLOT + j], ns), ns)]
        dst = buf_ref.at[sl * TOP_K + j, pl.ds(pl.multiple_of(t * ns, ns), ns)]
        return pltpu.make_async_copy(src, dst, sems.at[sl])

    def gather(dref, sl):
        def issue(t, carry):
            for j in range(TOP_K):
                row_copy(dref, t, j, sl).start()
            return carry
        lax.fori_loop(0, tm, issue, 0, unroll=DMA_LOOP_UNROLL)

    @pl.when(i == 0)
    def _():
        gather(dest_ref, 0)

    @pl.when(i + 1 < n)
    def _():
        gather(dest_next_ref, 1 - slot)

    def drain(t, carry):
        for j in range(TOP_K):
            row_copy(dest_ref, t, j, slot).wait()
        return carry

    lax.fori_loop(0, tm, drain, 0, unroll=DMA_LOOP_UNROLL)
    gates = gate_ref[...]
    gj = [jnp.broadcast_to(gates[:, j:j + 1], (tm, TILE_LANES)) for j in range(TOP_K)]
    gate_f = mod_ref[5:6, :]
    half = x_ref.shape[1] // 2
    for s in range(ns):
        acc_lo = jnp.zeros((tm, TILE_LANES), F32)
        acc_hi = jnp.zeros((tm, TILE_LANES), F32)
        for j in range(TOP_K):
            lo, hi = _unpack_rows(buf_ref.at[slot * TOP_K + j], s, tm, ns)
            acc_lo = acc_lo + gj[j] * lo
            acc_hi = acc_hi + gj[j] * hi
        for acc, off in ((acc_lo, s * TILE_LANES), (acc_hi, half + s * TILE_LANES)):
            sl = slice(off, off + TILE_LANES)
            o_ref[:, sl] = x_ref[:, sl] + gate_f[:, sl] * (acc + sh_ref[:, sl])


def combine(ys, dest_rows, gates, shared, x, mod3, seqs, *, tm=128):
    t, d = x.shape
    ns = d // (2 * TILE_LANES)
    n = t // tm
    tok = lambda i: (i, 0)
    return pl.pallas_call(
        functools.partial(_combine_kernel, tm=tm, ns=ns),
        grid=(n,),
        in_specs=[pl.BlockSpec((tm * SLOT,), lambda i: (i,), memory_space=pltpu.SMEM),
                  pl.BlockSpec((tm * SLOT,), lambda i: (jnp.minimum(i + 1, n - 1),), memory_space=pltpu.SMEM),
                  pl.BlockSpec(memory_space=pl.ANY),
                  pl.BlockSpec((tm, SLOT), tok),
                  pl.BlockSpec((tm, d), tok),
                  pl.BlockSpec((tm, d), tok),
                  pl.BlockSpec((None, 6, d), lambda i: (seqs.seq_of_tile(i, tm), 0, 0))],
        out_specs=pl.BlockSpec((tm, d), tok),
        out_shape=jax.ShapeDtypeStruct((t, d), F32),
        scratch_shapes=[pltpu.VMEM((2 * TOP_K, tm * ns, TILE_LANES), jnp.uint32),
                        pltpu.SemaphoreType.DMA((2,))],
        name="moe_combine",
        compiler_params=_params("arbitrary"),
    )(dest_rows, dest_rows, ys, gates, shared, x, mod3)


def moe_block(x, mod3, g, router_w, router_bias, w_gate, w_up, w_down, sh_gate, sh_up, sh_down, layer, seqs,
              *, bm=256):
    t, d = x.shape
    n_e = router_w.shape[1]
    hp, idx, gates, pos, counts = router(x, g, mod3, router_w, router_bias, seqs)
    cnt = counts[0].astype(jnp.int32)
    padded = (cnt + bm - 1) // bm * bm
    end_pad = jnp.cumsum(padded)
    start_pad = end_pad - padded
    ns = d // (2 * TILE_LANES)
    dest_rows = ((jnp.take(start_pad, idx, axis=0) + pos) * ns).reshape(-1)
    nblk = -(-(t * TOP_K) // bm) + n_e
    blk_start = jnp.arange(nblk, dtype=jnp.int32) * bm
    blk_expert = jnp.minimum(jnp.sum(end_pad[None, :] <= blk_start[:, None], axis=1), n_e - 1).astype(jnp.int32)
    n_used = (end_pad[-1:] // bm).astype(jnp.int32)
    last_blk = jnp.where(padded > 0, end_pad // bm - 1, -1)
    tail_blk = n_used[0] + jnp.arange(n_e, dtype=jnp.int32)
    zero_blocks = jnp.concatenate([last_blk, jnp.where(tail_blk < nblk, tail_blk, -1)]).astype(jnp.int32)
    sh = lambda w: cast_bf16(w[:, None], layer)[0]
    shared, xs = shared_ffn_dispatch(hp, dest_rows, zero_blocks, nblk * bm, sh(sh_gate), sh(sh_up), sh(sh_down),
                                     bm=bm)
    bf = lambda w: cast_bf16(w, layer)
    ys = expert_ffn(xs, blk_expert, n_used, bf(w_gate), bf(w_up), bf(w_down), bm=bm)
    return combine(ys, dest_rows, gates, shared, x, mod3, seqs)


def _final_norm_kernel(x_ref, g_ref, o_ref):
    x = x_ref[...]
    o_ref[...] = x * lax.rsqrt(jnp.mean(x * x, axis=-1, keepdims=True) + RMS_EPS) * g_ref[...]


def final_rmsnorm(x, g, row0, rows, *, tm=256):
    d = x.shape[1]
    b0 = row0 // tm
    return pl.pallas_call(
        _final_norm_kernel,
        grid=(rows // tm,),
        in_specs=[pl.BlockSpec((tm, d), lambda i: (b0 + i, 0)), pl.BlockSpec((1, d), lambda i: (0, 0))],
        out_specs=pl.BlockSpec((tm, d), lambda i: (i, 0)),
        out_shape=jax.ShapeDtypeStruct((rows, d), F32),
        name="final_norm",
        compiler_params=_params("parallel"),
    )(x, g)


def _block_diag2(w):
    k, n = w.shape[1:]
    z = jnp.zeros((k, n), w.dtype)
    return jnp.concatenate([jnp.concatenate([w[0], z], axis=1), jnp.concatenate([z, w[1]], axis=1)], axis=0)


def rwkv_block(x, mod3, g, seqs, mu, w_r, w_k, w_v, w_o, dec_w0, dec_w1, dec_w2, iclr_a0, iclr_a1, iclr_a2,
               gate_g1, gate_g2, k_k, k_a, r_k, gn_w, gn_b):
    d = x.shape[1]
    bf = lambda w: w.astype(BF16)
    xr, xw, xk, xv, xa, xg = shiftmix(x, g, mod3, mu, seqs)
    r = mm(xr, bf(w_r), out_dtype=F32, name="rwkv_proj")
    k = mm(xk, bf(w_k), out_dtype=F32, name="rwkv_proj")
    v = mm(xv, bf(w_v), out_dtype=F32, name="rwkv_proj")
    lw_in = mm(xw, bf(jnp.concatenate([dec_w1[0], dec_w1[1]], axis=1)), out_dtype=BF16, epi=_epi_tanh,
               name="rwkv_decay_lora_in")
    la_in = mm(xa, bf(jnp.concatenate([iclr_a1[0], iclr_a1[1]], axis=1)), out_dtype=BF16, name="rwkv_iclr_lora_in")
    lg_in = mm(xg, bf(gate_g1), out_dtype=BF16, epi=_epi_sigmoid, name="rwkv_gate_lora_in")
    gate = mm(lg_in, bf(gate_g2), out_dtype=F32, name="rwkv_gate_lora_out")
    lw2 = mm(lw_in, bf(_block_diag2(dec_w2)), out_dtype=F32, epi=_epi_log_decay,
             extras=[(dec_w0.reshape(1, 2 * d), "col")], name="rwkv_log_decay")
    a2 = mm(la_in, bf(_block_diag2(iclr_a2)), out_dtype=F32, epi=_epi_bias_sigmoid,
            extras=[(iclr_a0.reshape(1, 2 * d), "col")], name="rwkv_iclr")
    k_a2 = k_a.reshape(1, d)
    yf, yb = wkv(r, k, v, lw2, a2, k_k.reshape(1, d), k_a2, seqs)
    yo = rwkv_post(yf, yb, r, k, v, a2, gate, k_a2, r_k.reshape(1, d), gn_w.reshape(1, d), gn_b.reshape(1, d))
    return mm(yo, bf(w_o), out_dtype=F32, epi=_epi_residual(2), extras=[(x, "row"), (mod3, "seq")], seqs=seqs,
              name="mixer_out_proj")


def attention_block(x, mod3, g, seqs, w_qkv, w_o, sink):
    d = x.shape[1]
    h = modnorm(x, g, mod3, seqs, row=0)
    qkv = mm(h, w_qkv.astype(BF16), out_dtype=BF16, name="attn_qkv")
    o = window_attention(qkv, sink, seqs, d=d)
    return mm(o, w_o.astype(BF16), out_dtype=F32, epi=_epi_residual(2), extras=[(x, "row"), (mod3, "seq")],
              seqs=seqs, name="mixer_out_proj")


def kernel(x_prompt, x_sample, c_prompt, c_sample, ada_w, ada_b, norm_mix, norm_ffn, rw_mu, rw_wr, rw_wk, rw_wv, rw_wo, rw_dec_w0, rw_dec_w1, rw_dec_w2, rw_iclr_a0, rw_iclr_a1, rw_iclr_a2, rw_gate_g1, rw_gate_g2, rw_k_k, rw_k_a, rw_r_k, rw_gn_w, rw_gn_b, at_w_qkv, at_w_o, at_sink, router_w, router_bias, moe_w_gate, moe_w_up, moe_w_down, sh_w_gate, sh_w_up, sh_w_down, final_norm):
    b1, s1, d = x_prompt.shape
    b2, s2, _ = x_sample.shape
    seqs = Seqs(b1, s1, b2, s2)
    depth = ada_w.shape[0]
    x = jnp.concatenate([x_prompt.reshape(b1 * s1, d), x_sample.reshape(b2 * s2, d)], axis=0)
    c = jnp.concatenate([c_prompt, c_sample], axis=0)
    rows = -(-seqs.nseq // TILE_SUBLANES) * TILE_SUBLANES
    c_act = jnp.pad(jax.nn.silu(c), ((0, rows - seqs.nseq), (0, 0)))
    for i in range(depth):
        j = i // 2
        mod = mm(c_act, ada_w[i], out_dtype=F32, epi=_epi_bias, extras=[(ada_b[i].reshape(1, -1), "col")],
                 nb=d, precision=lax.Precision.HIGHEST, name="adaln_mod")
        mod3 = mod[:seqs.nseq].reshape(seqs.nseq, 6, d)
        g_mix = norm_mix[i].reshape(1, d)
        if i % 2 == 0:
            x = rwkv_block(x, mod3, g_mix, seqs, rw_mu[j], rw_wr[j], rw_wk[j], rw_wv[j], rw_wo[j], rw_dec_w0[j],
                           rw_dec_w1[j], rw_dec_w2[j], rw_iclr_a0[j], rw_iclr_a1[j], rw_iclr_a2[j],
                           rw_gate_g1[j], rw_gate_g2[j], rw_k_k[j], rw_k_a[j], rw_r_k[j], rw_gn_w[j], rw_gn_b[j])
        else:
            x = attention_block(x, mod3, g_mix, seqs, at_w_qkv[j], at_w_o[j], at_sink[j])
        x = moe_block(x, mod3, norm_ffn[i].reshape(1, d), router_w[i], router_bias[i].reshape(1, -1),
                      moe_w_gate, moe_w_up, moe_w_down, sh_w_gate, sh_w_up, sh_w_down, i, seqs)
    fw = final_norm.reshape(1, d)
    y1 = final_rmsnorm(x, fw, 0, seqs.t1).reshape(b1, s1, d)
    y2 = final_rmsnorm(x, fw, seqs.t1, b2 * s2).reshape(b2, s2, d)
    return (y1, y2)
```

```python
import functools
from typing import NamedTuple

import jax
import jax.numpy as jnp
import numpy as np
from jax import lax
from jax.experimental import pallas as pl
from jax.experimental.pallas import tpu as pltpu

F32 = jnp.float32
BF16 = jnp.bfloat16

HEAD = 64
GROUP_LANES = 256
CHUNK = 64
ATT_GROUP = 4
ATT_KV_BATCH = 1
WINDOW = 128
ATT_BLOCK = 128
TOP_K = 6
ROUTED_SCALE = 2.446
RMS_EPS = 1e-6
GN_EPS = 64e-5
NEG_INF = -1e30
SLOT = 8
DMA_LOOP_UNROLL = 8
TILE_SUBLANES = 8
TILE_LANES = 128
VMEM_LIMIT = 56 * 1024 * 1024


class Seqs(NamedTuple):
    b1: int
    s1: int
    b2: int
    s2: int

    @property
    def t1(self):
        return self.b1 * self.s1

    @property
    def total(self):
        return self.b1 * self.s1 + self.b2 * self.s2

    @property
    def nseq(self):
        return self.b1 + self.b2

    def seq_of_tile(self, i, tm):
        n1 = self.t1 // tm
        return jnp.where(i < n1, i // (self.s1 // tm), self.b1 + (i - n1) // (self.s2 // tm))

    def starts_seq(self, tok):
        return jnp.where(tok < self.t1, lax.rem(tok, self.s1) == 0, lax.rem(tok - self.t1, self.s2) == 0)

    def ends_seq(self, tok_end):
        return jnp.where(tok_end <= self.t1, lax.rem(tok_end, self.s1) == 0,
                         lax.rem(tok_end - self.t1, self.s2) == 0)


def _params(*sem):
    return pltpu.CompilerParams(dimension_semantics=sem, vmem_limit_bytes=VMEM_LIMIT)


def _modnorm(x, g, shift, scale):
    y = x * lax.rsqrt(jnp.mean(x * x, axis=-1, keepdims=True) + RMS_EPS)
    return (y * g) * (1.0 + scale) + shift


def _split_bf16(x):
    hi = x.astype(BF16)
    lo = (x - hi.astype(F32)).astype(BF16)
    return hi, lo


def _dot(a, b):
    return jnp.dot(a, b, preferred_element_type=F32)


def _dot_nt(a, b):
    return lax.dot_general(a, b, (((1,), (1,)), ((), ())), preferred_element_type=F32)


def _dot_tn(a, b):
    return lax.dot_general(a, b, (((0,), (0,)), ((), ())), preferred_element_type=F32)


def _block_mask():
    r = lax.broadcasted_iota(jnp.int32, (GROUP_LANES, GROUP_LANES), 0) // HEAD
    c = lax.broadcasted_iota(jnp.int32, (GROUP_LANES, GROUP_LANES), 1) // HEAD
    return r == c


def _group_sum(x, ones_bd):
    hi, lo = _split_bf16(x)
    return _dot(hi, ones_bd) + _dot(lo, ones_bd)


def _modnorm_kernel(x_ref, g_ref, mod_ref, o_ref, *, row):
    h = _modnorm(x_ref[...], g_ref[...], mod_ref[row:row + 1, :], mod_ref[row + 1:row + 2, :])
    o_ref[...] = h.astype(o_ref.dtype)


def modnorm(x, g, mod3, seqs, *, row, tm=256):
    t, d = x.shape
    return pl.pallas_call(
        functools.partial(_modnorm_kernel, row=row),
        grid=(t // tm,),
        in_specs=[pl.BlockSpec((tm, d), lambda i: (i, 0)),
                  pl.BlockSpec((1, d), lambda i: (0, 0)),
                  pl.BlockSpec((None, 6, d), lambda i: (seqs.seq_of_tile(i, tm), 0, 0))],
        out_specs=pl.BlockSpec((tm, d), lambda i: (i, 0)),
        out_shape=jax.ShapeDtypeStruct((t, d), BF16),
        name="modnorm",
        compiler_params=_params("parallel"),
    )(x, g, mod3)


def _shiftmix_kernel(x_ref, xp_ref, xn_ref, g_ref, mod_ref, mu_ref, *o_refs, seqs, tm):
    i = pl.program_id(0)
    d = x_ref.shape[1]
    first = seqs.starts_seq(i * tm)
    last = seqs.ends_seq((i + 1) * tm)

    def inv_rms(ref):
        x = ref[...]
        return lax.rsqrt(jnp.mean(x * x, axis=-1, keepdims=True) + RMS_EPS)

    inv, inv_p, inv_n = inv_rms(x_ref), inv_rms(xp_ref), inv_rms(xn_ref)
    rows = lax.broadcasted_iota(jnp.int32, (tm, GROUP_LANES), 0)
    for c in range(d // GROUP_LANES):
        sl = slice(c * GROUP_LANES, (c + 1) * GROUP_LANES)
        mod = lambda x, r: (x * r * g_ref[:, sl]) * (1.0 + mod_ref[1:2, sl]) + mod_ref[0:1, sl]
        h = mod(x_ref[:, sl], inv)
        hp = mod(xp_ref[:, sl], inv_p)[TILE_SUBLANES - 1:TILE_SUBLANES, :]
        hn = mod(xn_ref[:, sl], inv_n)[0:1, :]
        hp = jnp.where(first, 0.0, hp)
        hn = jnp.where(last, 0.0, hn)
        prev = jnp.where(rows == 0, hp, pltpu.roll(h, 1, 0))
        nxt = jnp.where(rows == tm - 1, hn, pltpu.roll(h, tm - 1, 0))
        xx = 0.5 * (prev + nxt) - h
        for j, o_ref in enumerate(o_refs):
            o_ref[:, sl] = (h + xx * mu_ref[j:j + 1, sl]).astype(o_ref.dtype)


def shiftmix(x, g, mod3, mu, seqs, *, tm=256):
    t, d = x.shape
    nb8 = t // TILE_SUBLANES
    r8 = tm // TILE_SUBLANES
    return pl.pallas_call(
        functools.partial(_shiftmix_kernel, seqs=seqs, tm=tm),
        grid=(t // tm,),
        in_specs=[pl.BlockSpec((tm, d), lambda i: (i, 0)),
                  pl.BlockSpec((TILE_SUBLANES, d), lambda i: (jnp.maximum(i * r8 - 1, 0), 0)),
                  pl.BlockSpec((TILE_SUBLANES, d), lambda i: (jnp.minimum((i + 1) * r8, nb8 - 1), 0)),
                  pl.BlockSpec((1, d), lambda i: (0, 0)),
                  pl.BlockSpec((None, 6, d), lambda i: (seqs.seq_of_tile(i, tm), 0, 0)),
                  pl.BlockSpec((6, d), lambda i: (0, 0))],
        out_specs=[pl.BlockSpec((tm, d), lambda i: (i, 0))] * 6,
        out_shape=[jax.ShapeDtypeStruct((t, d), BF16)] * 6,
        name="rwkv_shiftmix",
        compiler_params=_params("parallel"),
    )(x, x, x, g, mod3, mu)


def _cast_kernel(w_ref, o_ref):
    o_ref[...] = w_ref[...].astype(o_ref.dtype)


def cast_bf16(w, layer):
    _, e, k, n = w.shape
    return pl.pallas_call(
        _cast_kernel,
        grid=(e,),
        in_specs=[pl.BlockSpec((None, None, k, n), lambda i: (layer, i, 0, 0))],
        out_specs=pl.BlockSpec((None, k, n), lambda i: (i, 0, 0)),
        out_shape=jax.ShapeDtypeStruct((e, k, n), BF16),
        name="cast_bf16",
        compiler_params=_params("parallel"),
    )(w)


def _mm_kernel(a_ref, w_ref, *refs, epi, tn, n_extra, precision):
    extra = refs[:n_extra]
    o_ref = refs[n_extra]
    n = o_ref.shape[-1]
    a = a_ref[...]
    for c in range(n // tn):
        sl = slice(c * tn, (c + 1) * tn)
        acc = jnp.dot(a, w_ref[:, sl], preferred_element_type=F32, precision=precision)
        if epi is not None:
            acc = epi(acc, *[e[:, sl] for e in extra])
        o_ref[:, sl] = acc.astype(o_ref.dtype)


def mm(a, w, *, out_dtype, name, epi=None, extras=(), seqs=None, tm=512, tn=512, nb=None, precision=None):
    m, k = a.shape
    n = w.shape[1]
    nb = n if nb is None else nb
    tn = max([c for c in range(TILE_LANES, min(tn, nb) + 1, TILE_LANES) if nb % c == 0], default=nb)
    tm = min(tm, m)
    specs = [pl.BlockSpec((tm, k), lambda i, j: (i, 0)), pl.BlockSpec((k, nb), lambda i, j: (0, j))]
    args = [a, w]
    for arr, kind in extras:
        if kind == "col":
            specs.append(pl.BlockSpec((1, nb), lambda i, j: (0, j)))
        elif kind == "row":
            specs.append(pl.BlockSpec((tm, nb), lambda i, j: (i, j)))
        else:
            specs.append(pl.BlockSpec((None, arr.shape[1], nb),
                                      lambda i, j: (seqs.seq_of_tile(i, tm), 0, j)))
        args.append(arr)
    return pl.pallas_call(
        functools.partial(_mm_kernel, epi=epi, tn=tn, n_extra=len(extras), precision=precision),
        grid=(m // tm, n // nb),
        in_specs=specs,
        out_specs=pl.BlockSpec((tm, nb), lambda i, j: (i, j)),
        out_shape=jax.ShapeDtypeStruct((m, n), out_dtype),
        name=name,
        compiler_params=_params("parallel", "arbitrary"),
    )(*args)


def _epi_bias(acc, b):
    return acc + b


def _epi_tanh(acc):
    return jnp.tanh(acc)


def _epi_sigmoid(acc):
    return jax.nn.sigmoid(acc)


def _sigmoid(x):
    return 0.5 * jnp.tanh(0.5 * x) + 0.5


def _epi_log_decay(acc, w0):
    return -float(np.exp(-0.5)) * _sigmoid(acc + w0)


def _epi_bias_sigmoid(acc, b):
    return _sigmoid(acc + b)


def _epi_residual(row):
    def epi(acc, x, mod):
        return x + mod[row:row + 1, :] * acc
    return epi


def _expand_bd(m, bm):
    mb = m.astype(BF16)
    return jnp.where(bm, jnp.concatenate([mb] * (GROUP_LANES // HEAD), axis=0), jnp.zeros((), BF16))


def _wkv_chunks(chains, bm, ones_bd):
    c = CHUNK
    n = len(chains)
    rev = [ch[8] for ch in chains]
    t_i = lax.broadcasted_iota(jnp.int32, (c, c), 0)
    s_i = lax.broadcasted_iota(jnp.int32, (c, c), 1)
    tri = {False: (s_i <= t_i).astype(BF16), True: (s_i >= t_i).astype(BF16)}
    t2 = lax.broadcasted_iota(jnp.int32, (c, GROUP_LANES), 0)
    s2 = lax.broadcasted_iota(jnp.int32, (c, GROUP_LANES), 1) % c
    strict = {False: s2 < t2, True: s2 > t2}
    incl = {False: s2 <= t2, True: s2 >= t2}
    eye = jnp.where(s2 == t2, 1.0, 0.0)

    k2 = [ch[1] * ch[5] for ch in chains]
    sq = [piece for q in k2 for piece in _split_bf16(q * q)]
    ss_all = _dot(jnp.concatenate(sq, axis=0), ones_bd)
    ss = [ss_all[2 * j * c:(2 * j + 1) * c] + ss_all[(2 * j + 1) * c:(2 * j + 2) * c] for j in range(n)]
    kk = [q / jnp.maximum(jnp.sqrt(s), 1e-12) for q, s in zip(k2, ss)]
    kd = [ch[1] * (1.0 + (ch[4] - 1.0) * ch[6]) for ch in chains]
    cw = [None] * n
    for rv in (False, True):
        ids = [j for j in range(n) if rev[j] == rv]
        if ids:
            lw = jnp.concatenate([piece for j in ids for piece in _split_bf16(chains[j][3])], axis=1)
            cw_all = _dot(tri[rv], lw)
            for pos, j in enumerate(ids):
                base = 2 * pos * GROUP_LANES
                cw[j] = (cw_all[:, base:base + GROUP_LANES]
                         + cw_all[:, base + GROUP_LANES:base + 2 * GROUP_LANES])
    edge = [w[0:1, :] if rv else w[c - 1:c, :] for w, rv in zip(cw, rev)]
    e_neg = [jnp.exp(-w) for w in cw]
    al = [-q * jnp.exp(w - ch[3]) for q, w, ch in zip(kk, cw, chains)]
    rb = [ch[0] * jnp.exp(w) for w, ch in zip(cw, chains)]
    ka = [q * ch[4] for q, ch in zip(kk, chains)]
    be = [q * e for q, e in zip(ka, e_neg)]
    kb = [q * e for q, e in zip(kd, e_neg)]
    x = [jnp.concatenate([p, q], axis=0).astype(BF16) for p, q in zip(al, rb)]
    z_bd = [jnp.concatenate([_expand_bd(p, bm), _expand_bd(q, bm)], axis=0) for p, q in zip(be, kb)]
    a_all = [_dot_nt(p, q) for p, q in zip(x, z_bd)]
    a_ab = [jnp.where(strict[rv], m[:c, :GROUP_LANES], 0.0) for m, rv in zip(a_all, rev)]
    a_kr = [jnp.concatenate([jnp.where(strict[rv], m[:c, GROUP_LANES:], 0.0),
                             jnp.where(incl[rv], m[c:, GROUP_LANES:], 0.0)], axis=0).astype(BF16)
            for m, rv in zip(a_all, rev)]
    a_rb = [jnp.where(incl[rv], m[c:, :GROUP_LANES], 0.0).astype(BF16) for m, rv in zip(a_all, rev)]
    p = [eye + m for m in a_ab]
    ap = [_dot(m.astype(BF16), _expand_bd(m, bm)) for m in a_ab]
    for _ in range(4):
        both = [_dot(jnp.concatenate([m, q], axis=0).astype(BF16), _expand_bd(m, bm)) for m, q in zip(ap, p)]
        p = [q + b[c:] for q, b in zip(p, both)]
        ap = [b[:c] for b in both]
    p = [q + _dot(q.astype(BF16), _expand_bd(m, bm)) for q, m in zip(p, ap)]
    xh = [_dot_nt(q, ch[7].astype(BF16)) for q, ch in zip(x, chains)]
    av = [_dot(m, _expand_bd(ch[2], bm)) for m, ch in zip(a_kr, chains)]
    u = [_dot(q.astype(BF16), _expand_bd(h[:c] + w[:c], bm)) for q, h, w in zip(p, xh, av)]
    y = [h[c:] + w[c:] + _dot(m, _expand_bd(q, bm)) for h, w, m, q in zip(xh, av, a_rb, u)]
    e_rest = [jnp.exp(e - w) for e, w in zip(edge, cw)]
    zw = [jnp.concatenate([p_ * e, q * e], axis=0).astype(BF16) for p_, q, e in zip(ka, kd, e_rest)]
    uv = [jnp.concatenate([q, ch[2]], axis=0).astype(BF16) for q, ch in zip(u, chains)]
    upd = [_dot_tn(p_, q) for p_, q in zip(uv, zw)]
    ht_new = [chains[j][7] * jnp.exp(edge[j]) + jnp.where(bm, upd[j], 0.0) for j in range(n)]
    return y, ht_new


def _wkv_kernel(rf, kf, vf, lwf, af, rb, kb, vb, lwb, ab, kk_ref, ka_ref, yf_ref, yb_ref, st_ref,
                *, seqs, ct, ngb):
    i = pl.program_id(1)
    nt = pl.num_programs(1)
    nc = ct // CHUNK
    bm = _block_mask()
    ones_bd = bm.astype(BF16)

    def load(refs, lanes, off, st_idx, reset, reverse):
        rows = pl.ds(off, CHUNK)
        r, k, v, lw, a = [ref[rows, lanes] for ref in refs]
        ht = jnp.where(reset, 0.0, st_ref[st_idx])
        return (r, k, v, lw, a, kk_ref[:, lanes], ka_ref[:, lanes], ht, reverse)

    def body(c, carry):
        off_f = pl.multiple_of(c * CHUNK, CHUNK)
        off_b = pl.multiple_of((nc - 1 - c) * CHUNK, CHUNK)
        tok_b = (nt - 1 - i) * ct + off_b
        reset_f = seqs.starts_seq(i * ct + off_f)
        reset_b = seqs.ends_seq(tok_b + CHUNK)
        lanes = [slice(g * GROUP_LANES, (g + 1) * GROUP_LANES) for g in range(ngb)]
        chains = [load((rf, kf, vf, lwf, af), lanes[g], off_f, g, reset_f, False) for g in range(ngb)]
        chains += [load((rb, kb, vb, lwb, ab), lanes[g], off_b, ngb + g, reset_b, True) for g in range(ngb)]
        y, ht = _wkv_chunks(chains, bm, ones_bd)
        for g in range(ngb):
            st_ref[g] = ht[g]
            st_ref[ngb + g] = ht[ngb + g]
            yf_ref[pl.ds(off_f, CHUNK), lanes[g]] = y[g]
            yb_ref[pl.ds(off_b, CHUNK), lanes[g]] = y[ngb + g]
        return carry

    lax.fori_loop(0, nc, body, 0)


def wkv(r, k, v, lw2, a2, k_k, k_a, seqs, *, ct=128, ngb=8):
    t, d = r.shape
    ngb = min(ngb, d // GROUP_LANES)
    wl = ngb * GROUP_LANES
    nlb = d // wl
    nt = t // ct
    fwd = lambda g, i: (i, g)
    bwd = lambda g, i: (nt - 1 - i, g)
    bwd2 = lambda g, i: (nt - 1 - i, nlb + g)
    blk = (ct, wl)
    par = pl.BlockSpec((1, wl), lambda g, i: (0, g))
    return pl.pallas_call(
        functools.partial(_wkv_kernel, seqs=seqs, ct=ct, ngb=ngb),
        grid=(nlb, nt),
        in_specs=[pl.BlockSpec(blk, fwd)] * 5 + [pl.BlockSpec(blk, bwd)] * 3 + [pl.BlockSpec(blk, bwd2)] * 2
                 + [par, par],
        out_specs=[pl.BlockSpec(blk, fwd), pl.BlockSpec(blk, bwd)],
        out_shape=[jax.ShapeDtypeStruct((t, d), F32)] * 2,
        scratch_shapes=[pltpu.VMEM((2 * ngb, GROUP_LANES, GROUP_LANES), F32)],
        name="wkv_scan",
        compiler_params=_params("parallel", "arbitrary"),
    )(r, k, v, lw2, a2, r, k, v, lw2, a2, k_k, k_a)


def _rwkv_post_kernel(yf, yb, r, k, v, af, ab, g, ka, rk, gw, gb, o_ref):
    ones_bd = _block_mask().astype(BF16)
    inv_n = 1.0 / HEAD
    for c in range(o_ref.shape[1] // GROUP_LANES):
        sl = slice(c * GROUP_LANES, (c + 1) * GROUP_LANES)
        y = yf[:, sl] + yb[:, sl]
        mean = _group_sum(y, ones_bd) * inv_n
        yc = y - mean
        var = _group_sum(yc * yc, ones_bd) * inv_n
        yn = yc * lax.rsqrt(var + GN_EPS) * gw[:, sl] + gb[:, sl]
        kv = k[:, sl]
        k_a = ka[:, sl]
        kd_sum = kv * (1.0 + (af[:, sl] - 1.0) * k_a) + kv * (1.0 + (ab[:, sl] - 1.0) * k_a)
        bonus = _group_sum(r[:, sl] * kd_sum * rk[:, sl], ones_bd) * v[:, sl]
        o_ref[:, sl] = ((yn + bonus) * g[:, sl]).astype(o_ref.dtype)


def rwkv_post(yf, yb, r, k, v, a2, g, k_a, r_k, gn_w, gn_b, *, tm=512, lanes=1024):
    t, d = r.shape
    lanes = min(lanes, d)
    ng = d // lanes
    blk = pl.BlockSpec((tm, lanes), lambda i, j: (i, j))
    blk_b = pl.BlockSpec((tm, lanes), lambda i, j: (i, ng + j))
    par = pl.BlockSpec((1, lanes), lambda i, j: (0, j))
    return pl.pallas_call(
        _rwkv_post_kernel,
        grid=(t // tm, ng),
        in_specs=[blk] * 5 + [blk, blk_b, blk] + [par] * 4,
        out_specs=blk,
        out_shape=jax.ShapeDtypeStruct((t, d), BF16),
        name="rwkv_post",
        compiler_params=_params("parallel", "parallel"),
    )(yf, yb, r, k, v, a2, a2, g, k_a, r_k, gn_w, gn_b)


def _attn_kernel(sink_ref, q_ref, kp_ref, kc_ref, kn_ref, vp_ref, vc_ref, vn_ref, o_ref, *, seqs, n_heads):
    i = pl.program_id(0)
    blk = ATT_BLOCK
    span = blk + 2 * WINDOW
    first = seqs.starts_seq(i * blk)
    last = seqs.ends_seq((i + 1) * blk)
    qi = lax.broadcasted_iota(jnp.int32, (blk, span), 0)
    ci = lax.broadcasted_iota(jnp.int32, (blk, span), 1)
    dist_i = jnp.abs(qi - (ci - WINDOW))
    valid = (dist_i <= WINDOW) & ((ci >= WINDOW) | jnp.logical_not(first)) \
        & ((ci < WINDOW + blk) | jnp.logical_not(last))
    dist = dist_i.astype(F32)
    n_kv = n_heads // ATT_GROUP
    kv_batch = min(ATT_KV_BATCH, n_kv)
    outs = []
    for kv0 in range(0, n_kv, kv_batch):
        heads = [(kvh, kvh * ATT_GROUP + g) for kvh in range(kv0, kv0 + kv_batch) for g in range(ATT_GROUP)]
        k_span, v_span = {}, {}
        for kvh in range(kv0, kv0 + kv_batch):
            ks = slice(kvh * HEAD, (kvh + 1) * HEAD)
            k_span[kvh] = jnp.concatenate([kp_ref[:, ks], kc_ref[:, ks], kn_ref[:, ks]], axis=0)
            v_span[kvh] = jnp.concatenate([vp_ref[:, ks], vc_ref[:, ks], vn_ref[:, ks]], axis=0)
        s = [_dot_nt(q_ref[:, h * HEAD:(h + 1) * HEAD], k_span[kvh]) for kvh, h in heads]
        s = [jnp.where(valid, sc * (HEAD ** -0.5) - (2.0 ** (-8.0 * (h + 1) / n_heads)) * dist, NEG_INF)
             for sc, (_, h) in zip(s, heads)]
        m = [jnp.maximum(jnp.max(sc, axis=-1, keepdims=True), sink_ref[h]) for sc, (_, h) in zip(s, heads)]
        e = [jnp.exp(sc - mx) for sc, mx in zip(s, m)]
        denom = [jnp.sum(ex, axis=-1, keepdims=True) + jnp.exp(sink_ref[h] - mx)
                 for ex, mx, (_, h) in zip(e, m, heads)]
        pv = [_dot(ex.astype(BF16), v_span[kvh]) for ex, (kvh, _) in zip(e, heads)]
        outs += [o / dn for o, dn in zip(pv, denom)]
    o_ref[...] = jnp.concatenate(outs, axis=-1).astype(o_ref.dtype)


def window_attention(qkv, sink, seqs, *, d):
    t = qkv.shape[0]
    n_heads = d // HEAD
    kvw = (n_heads // ATT_GROUP) * HEAD
    nb = t // ATT_BLOCK
    kcol = d // kvw
    prev = lambda i: jnp.maximum(i - 1, 0)
    nxt = lambda i: jnp.minimum(i + 1, nb - 1)
    kv_specs = [pl.BlockSpec((ATT_BLOCK, kvw), lambda i, c=c, f=f: (f(i), c))
                for c in (kcol, kcol + 1) for f in (prev, lambda i: i, nxt)]
    return pl.pallas_call(
        functools.partial(_attn_kernel, seqs=seqs, n_heads=n_heads),
        grid=(nb,),
        in_specs=[pl.BlockSpec(memory_space=pltpu.SMEM),
                  pl.BlockSpec((ATT_BLOCK, d), lambda i: (i, 0))] + kv_specs,
        out_specs=pl.BlockSpec((ATT_BLOCK, d), lambda i: (i, 0)),
        out_shape=jax.ShapeDtypeStruct((t, d), BF16),
        name="window_attention",
        compiler_params=_params("parallel"),
    )(sink, qkv, qkv, qkv, qkv, qkv, qkv, qkv)


def _pack_rows(ref, y):
    m = y.shape[0]
    half = y.shape[1] // 2
    ns = half // TILE_LANES
    lo = pltpu.bitcast(y[:, :half].astype(BF16).astype(F32), jnp.uint32) >> 16
    hi = pltpu.bitcast(y[:, half:].astype(BF16).astype(F32), jnp.uint32) & jnp.uint32(0xFFFF0000)
    w = lo | hi
    for s in range(ns):
        ref[pl.ds(s, m, stride=ns), :] = w[:, s * TILE_LANES:(s + 1) * TILE_LANES]


def _unpack_rows(ref, s, m, ns):
    w = ref[pl.ds(s, m, stride=ns), :]
    lo = pltpu.bitcast(w << 16, F32)
    hi = pltpu.bitcast(w & jnp.uint32(0xFFFF0000), F32)
    return lo, hi


def _router_kernel(x_ref, g_ref, mod_ref, rw_ref, rb_ref, hp_ref, idx_ref, gate_ref, pos_ref, cnt_ref,
                   run_ref, *, tm):
    i = pl.program_id(0)

    @pl.when(i == 0)
    def _():
        run_ref[...] = jnp.zeros_like(run_ref)

    h = _modnorm(x_ref[...], g_ref[...], mod_ref[3:4, :], mod_ref[4:5, :])
    _pack_rows(hp_ref, h)
    h_hi, h_lo = _split_bf16(h)
    w_hi, w_lo = _split_bf16(rw_ref[...])
    logits = _dot(h_hi, w_hi) + (_dot(h_hi, w_lo) + _dot(h_lo, w_hi))
    scores = jax.nn.sigmoid(logits)
    n_e = scores.shape[1]
    lane = lax.broadcasted_iota(jnp.int32, scores.shape, 1).astype(F32)
    work = scores + rb_ref[...]
    sel = jnp.zeros_like(scores)
    hot, idx, gates = [], [], []
    for _ in range(TOP_K):
        m = jnp.max(work, axis=-1, keepdims=True)
        first = jnp.min(jnp.where(work == m, lane, float(n_e)), axis=-1, keepdims=True)
        one = lane == first
        work = jnp.where(one, -jnp.inf, work)
        hot.append(one)
        idx.append(first)
        gates.append(jnp.sum(jnp.where(one, scores, 0.0), axis=-1, keepdims=True))
        sel = sel + one.astype(F32)
    total = gates[0]
    for gj in gates[1:]:
        total = total + gj
    t_i = lax.broadcasted_iota(jnp.int32, (tm, tm), 0)
    s_i = lax.broadcasted_iota(jnp.int32, (tm, tm), 1)
    before = _dot((s_i < t_i).astype(BF16), sel.astype(BF16)) + run_ref[...]
    run_ref[...] = run_ref[...] + jnp.sum(sel, axis=0, keepdims=True)
    cnt_ref[...] = run_ref[...]
    slot = lax.broadcasted_iota(jnp.int32, (tm, SLOT), 1)
    idx_o = jnp.zeros((tm, SLOT), jnp.int32)
    gate_o = jnp.zeros((tm, SLOT), F32)
    pos_o = jnp.zeros((tm, SLOT), jnp.int32)
    for j in range(TOP_K):
        pos_j = jnp.sum(jnp.where(hot[j], before, 0.0), axis=-1, keepdims=True).astype(jnp.int32)
        idx_o = jnp.where(slot == j, idx[j].astype(jnp.int32), idx_o)
        gate_o = jnp.where(slot == j, gates[j] / total * ROUTED_SCALE, gate_o)
        pos_o = jnp.where(slot == j, pos_j, pos_o)
    idx_ref[...] = idx_o
    gate_ref[...] = gate_o
    pos_ref[...] = pos_o


def router(x, g, mod3, router_w, router_bias, seqs, *, tm=256):
    t, d = x.shape
    n_e = router_w.shape[1]
    ns = d // (2 * TILE_LANES)
    tok = lambda i: (i, 0)
    return pl.pallas_call(
        functools.partial(_router_kernel, tm=tm),
        grid=(t // tm,),
        in_specs=[pl.BlockSpec((tm, d), tok),
                  pl.BlockSpec((1, d), lambda i: (0, 0)),
                  pl.BlockSpec((None, 6, d), lambda i: (seqs.seq_of_tile(i, tm), 0, 0)),
                  pl.BlockSpec((d, n_e), lambda i: (0, 0)),
                  pl.BlockSpec((1, n_e), lambda i: (0, 0))],
        out_specs=[pl.BlockSpec((tm * ns, TILE_LANES), tok),
                   pl.BlockSpec((tm, SLOT), tok), pl.BlockSpec((tm, SLOT), tok), pl.BlockSpec((tm, SLOT), tok),
                   pl.BlockSpec((1, n_e), lambda i: (0, 0))],
        out_shape=[jax.ShapeDtypeStruct((t * ns, TILE_LANES), jnp.uint32),
                   jax.ShapeDtypeStruct((t, SLOT), jnp.int32),
                   jax.ShapeDtypeStruct((t, SLOT), F32),
                   jax.ShapeDtypeStruct((t, SLOT), jnp.int32),
                   jax.ShapeDtypeStruct((1, n_e), F32)],
        scratch_shapes=[pltpu.VMEM((1, n_e), F32)],
        name="moe_router",
        compiler_params=_params("arbitrary"),
    )(x, g, mod3, router_w, router_bias)


def _ffn_rows(xs_ref, wg_ref, wu_ref, wd_ref, xb_ref):
    bm = xb_ref.shape[0]
    half = xb_ref.shape[1] // 2
    ns = half // TILE_LANES
    for s in range(ns):
        lo, hi = _unpack_rows(xs_ref, s, bm, ns)
        xb_ref[:, s * TILE_LANES:(s + 1) * TILE_LANES] = lo.astype(BF16)
        xb_ref[:, half + s * TILE_LANES:half + (s + 1) * TILE_LANES] = hi.astype(BF16)
    xb = xb_ref[...]
    hg = _dot(xb, wg_ref[...])
    hu = _dot(xb, wu_ref[...])
    hb = (hg * jax.nn.sigmoid(hg) * hu).astype(BF16)
    return _dot(hb, wd_ref[...])


def _shared_dispatch_kernel(dest_ref, zblk_ref, hp_ref, wg_ref, wu_ref, wd_ref, o_ref, xs_ref, xb_ref, zbuf_ref,
                            sem, zsem, *, tm, ns, bm):
    @pl.when(pl.program_id(0) == 0)
    def _():
        zbuf_ref[...] = jnp.zeros_like(zbuf_ref)

        def zero_copy(k):
            row0 = pl.multiple_of(zblk_ref[k] * (bm * ns), bm * ns)
            return pltpu.make_async_copy(zbuf_ref, xs_ref.at[pl.ds(row0, bm * ns)], zsem)

        def zero_issue(k, carry):
            @pl.when(zblk_ref[k] >= 0)
            def _():
                zero_copy(k).start()
            return carry

        def zero_drain(k, carry):
            @pl.when(zblk_ref[k] >= 0)
            def _():
                zero_copy(k).wait()
            return carry

        lax.fori_loop(0, zblk_ref.shape[0], zero_issue, 0)
        lax.fori_loop(0, zblk_ref.shape[0], zero_drain, 0)

    def row_copy(t, j):
        src = hp_ref.at[pl.ds(pl.multiple_of(t * ns, ns), ns)]
        dst = xs_ref.at[pl.ds(pl.multiple_of(dest_ref[t * SLOT + j], ns), ns)]
        return pltpu.make_async_copy(src, dst, sem)

    def issue(t, carry):
        for j in range(TOP_K):
            row_copy(t, j).start(priority=j % 2)
        return carry

    def drain(t, carry):
        for j in range(TOP_K):
            row_copy(t, j).wait()
        return carry

    lax.fori_loop(0, tm, issue, 0, unroll=DMA_LOOP_UNROLL)
    o_ref[...] = _ffn_rows(hp_ref, wg_ref, wu_ref, wd_ref, xb_ref)
    lax.fori_loop(0, tm, drain, 0, unroll=DMA_LOOP_UNROLL)


def shared_ffn_dispatch(hp, dest_rows, zero_blocks, n_rows, w_a, w_b, w_down, *, bm, tm=256):
    f, d = w_down.shape
    ns = d // (2 * TILE_LANES)
    t = hp.shape[0] // ns
    const = lambda i: (0, 0)
    return pl.pallas_call(
        functools.partial(_shared_dispatch_kernel, tm=tm, ns=ns, bm=bm),
        grid=(t // tm,),
        in_specs=[pl.BlockSpec((tm * SLOT,), lambda i: (i,), memory_space=pltpu.SMEM),
                  pl.BlockSpec(memory_space=pltpu.SMEM),
                  pl.BlockSpec((tm * ns, TILE_LANES), lambda i: (i, 0)),
                  pl.BlockSpec(w_a.shape, const), pl.BlockSpec(w_b.shape, const), pl.BlockSpec((f, d), const)],
        out_specs=[pl.BlockSpec((tm, d), lambda i: (i, 0)), pl.BlockSpec(memory_space=pl.ANY)],
        out_shape=[jax.ShapeDtypeStruct((t, d), F32),
                   jax.ShapeDtypeStruct((n_rows * ns, TILE_LANES), jnp.uint32)],
        scratch_shapes=[pltpu.VMEM((tm, d), BF16), pltpu.VMEM((bm * ns, TILE_LANES), jnp.uint32),
                        pltpu.SemaphoreType.DMA(()), pltpu.SemaphoreType.DMA(())],
        name="moe_shared_expert_dispatch",
        compiler_params=_params("arbitrary"),
    )(dest_rows, zero_blocks, hp, w_a, w_b, w_down)


def _expert_kernel(be_ref, nblk_ref, xs_ref, wg_ref, wu_ref, wd_ref, o_ref, xb_ref):
    i = pl.program_id(0)

    @pl.when(i < nblk_ref[0])
    def _():
        _pack_rows(o_ref, _ffn_rows(xs_ref, wg_ref, wu_ref, wd_ref, xb_ref))

    @pl.when(i >= nblk_ref[0])
    def _():
        o_ref[...] = jnp.zeros_like(o_ref)


def expert_ffn(xs, blk_expert, n_used, w_a, w_b, w_down, *, bm):
    f, d = w_down.shape[1:]
    ns = d // (2 * TILE_LANES)
    n_rows = xs.shape[0] // ns
    rows = pl.BlockSpec((bm * ns, TILE_LANES), lambda i, be, nb: (i, 0))
    return pl.pallas_call(
        _expert_kernel,
        grid_spec=pltpu.PrefetchScalarGridSpec(
            num_scalar_prefetch=2,
            grid=(n_rows // bm,),
            in_specs=[rows,
                      pl.BlockSpec((None,) + w_a.shape[1:], lambda i, be, nb: (be[i], 0, 0)),
                      pl.BlockSpec((None,) + w_b.shape[1:], lambda i, be, nb: (be[i], 0, 0)),
                      pl.BlockSpec((None, f, d), lambda i, be, nb: (be[i], 0, 0))],
            out_specs=rows,
            scratch_shapes=[pltpu.VMEM((bm, d), BF16)]),
        out_shape=jax.ShapeDtypeStruct((n_rows * ns, TILE_LANES), jnp.uint32),
        name="moe_experts",
        compiler_params=_params("arbitrary"),
    )(blk_expert, n_used, xs, w_a, w_b, w_down)


def _combine_kernel(dest_ref, dest_next_ref, ys_ref, gate_ref, sh_ref, x_ref, mod_ref, o_ref, buf_ref, sems,
                    *, tm, ns):
    i = pl.program_id(0)
    n = pl.num_programs(0)
    slot = lax.rem(i, 2)

    def row_copy(dref, t, j, sl):
        src = ys_ref.at[pl.ds(pl.multiple_of(dref[t * SLOT + j], ns), ns)]
        dst = buf_ref.at[sl * TOP_K + j, pl.ds(pl.multiple_of(t * ns, ns), ns)]
        return pltpu.make_async_copy(src, dst, sems.at[sl])

    def gather(dref, sl):
        def issue(t, carry):
            for j in range(TOP_K):
                row_copy(dref, t, j, sl).start()
            return carry
        lax.fori_loop(0, tm, issue, 0, unroll=DMA_LOOP_UNROLL)

    @pl.when(i == 0)
    def _():
        gather(dest_ref, 0)

    @pl.when(i + 1 < n)
    def _():
        gather(dest_next_ref, 1 - slot)

    def drain(t, carry):
        for j in range(TOP_K):
            row_copy(dest_ref, t, j, slot).wait()
        return carry

    lax.fori_loop(0, tm, drain, 0, unroll=DMA_LOOP_UNROLL)
    gates = gate_ref[...]
    gj = [jnp.broadcast_to(gates[:, j:j + 1], (tm, TILE_LANES)) for j in range(TOP_K)]
    gate_f = mod_ref[5:6, :]
    half = x_ref.shape[1] // 2
    for s in range(ns):
        acc_lo = jnp.zeros((tm, TILE_LANES), F32)
        acc_hi = jnp.zeros((tm, TILE_LANES), F32)
        for j in range(TOP_K):
            lo, hi = _unpack_rows(buf_ref.at[slot * TOP_K + j], s, tm, ns)
            acc_lo = acc_lo + gj[j] * lo
            acc_hi = acc_hi + gj[j] * hi
        for acc, off in ((acc_lo, s * TILE_LANES), (acc_hi, half + s * TILE_LANES)):
            sl = slice(off, off + TILE_LANES)
            o_ref[:, sl] = x_ref[:, sl] + gate_f[:, sl] * (acc + sh_ref[:, sl])


def combine(ys, dest_rows, gates, shared, x, mod3, seqs, *, tm=128):
    t, d = x.shape
    ns = d // (2 * TILE_LANES)
    n = t // tm
    tok = lambda i: (i, 0)
    return pl.pallas_call(
        functools.partial(_combine_kernel, tm=tm, ns=ns),
        grid=(n,),
        in_specs=[pl.BlockSpec((tm * SLOT,), lambda i: (i,), memory_space=pltpu.SMEM),
                  pl.BlockSpec((tm * SLOT,), lambda i: (jnp.minimum(i + 1, n - 1),), memory_space=pltpu.SMEM),
                  pl.BlockSpec(memory_space=pl.ANY),
                  pl.BlockSpec((tm, SLOT), tok),
                  pl.BlockSpec((tm, d), tok),
                  pl.BlockSpec((tm, d), tok),
                  pl.BlockSpec((None, 6, d), lambda i: (seqs.seq_of_tile(i, tm), 0, 0))],
        out_specs=pl.BlockSpec((tm, d), tok),
        out_shape=jax.ShapeDtypeStruct((t, d), F32),
        scratch_shapes=[pltpu.VMEM((2 * TOP_K, tm * ns, TILE_LANES), jnp.uint32),
                        pltpu.SemaphoreType.DMA((2,))],
        name="moe_combine",
        compiler_params=_params("arbitrary"),
    )(dest_rows, dest_rows, ys, gates, shared, x, mod3)


def moe_block(x, mod3, g, router_w, router_bias, w_gate, w_up, w_down, sh_gate, sh_up, sh_down, layer, seqs,
              *, bm=256):
    t, d = x.shape
    n_e = router_w.shape[1]
    hp, idx, gates, pos, counts = router(x, g, mod3, router_w, router_bias, seqs)
    cnt = counts[0].astype(jnp.int32)
    padded = (cnt + bm - 1) // bm * bm
    end_pad = jnp.cumsum(padded)
    start_pad = end_pad - padded
    ns = d // (2 * TILE_LANES)
    dest_rows = ((jnp.take(start_pad, idx, axis=0) + pos) * ns).reshape(-1)
    nblk = -(-(t * TOP_K) // bm) + n_e
    blk_start = jnp.arange(nblk, dtype=jnp.int32) * bm
    blk_expert = jnp.minimum(jnp.sum(end_pad[None, :] <= blk_start[:, None], axis=1), n_e - 1).astype(jnp.int32)
    n_used = (end_pad[-1:] // bm).astype(jnp.int32)
    last_blk = jnp.where(padded > 0, end_pad // bm - 1, -1)
    tail_blk = n_used[0] + jnp.arange(n_e, dtype=jnp.int32)
    zero_blocks = jnp.concatenate([last_blk, jnp.where(tail_blk < nblk, tail_blk, -1)]).astype(jnp.int32)
    sh = lambda w: cast_bf16(w[:, None], layer)[0]
    shared, xs = shared_ffn_dispatch(hp, dest_rows, zero_blocks, nblk * bm, sh(sh_gate), sh(sh_up), sh(sh_down),
                                     bm=bm)
    bf = lambda w: cast_bf16(w, layer)
    ys = expert_ffn(xs, blk_expert, n_used, bf(w_gate), bf(w_up), bf(w_down), bm=bm)
    return combine(ys, dest_rows, gates, shared, x, mod3, seqs)


def _final_norm_kernel(x_ref, g_ref, o_ref):
    x = x_ref[...]
    o_ref[...] = x * lax.rsqrt(jnp.mean(x * x, axis=-1, keepdims=True) + RMS_EPS) * g_ref[...]


def final_rmsnorm(x, g, row0, rows, *, tm=256):
    d = x.shape[1]
    b0 = row0 // tm
    return pl.pallas_call(
        _final_norm_kernel,
        grid=(rows // tm,),
        in_specs=[pl.BlockSpec((tm, d), lambda i: (b0 + i, 0)), pl.BlockSpec((1, d), lambda i: (0, 0))],
        out_specs=pl.BlockSpec((tm, d), lambda i: (i, 0)),
        out_shape=jax.ShapeDtypeStruct((rows, d), F32),
        name="final_norm",
        compiler_params=_params("parallel"),
    )(x, g)


def _block_diag2(w):
    k, n = w.shape[1:]
    z = jnp.zeros((k, n), w.dtype)
    return jnp.concatenate([jnp.concatenate([w[0], z], axis=1), jnp.concatenate([z, w[1]], axis=1)], axis=0)


def rwkv_block(x, mod3, g, seqs, mu, w_r, w_k, w_v, w_o, dec_w0, dec_w1, dec_w2, iclr_a0, iclr_a1, iclr_a2,
               gate_g1, gate_g2, k_k, k_a, r_k, gn_w, gn_b):
    d = x.shape[1]
    bf = lambda w: w.astype(BF16)
    xr, xw, xk, xv, xa, xg = shiftmix(x, g, mod3, mu, seqs)
    r = mm(xr, bf(w_r), out_dtype=F32, name="rwkv_proj")
    k = mm(xk, bf(w_k), out_dtype=F32, name="rwkv_proj")
    v = mm(xv, bf(w_v), out_dtype=F32, name="rwkv_proj")
    lw_in = mm(xw, bf(jnp.concatenate([dec_w1[0], dec_w1[1]], axis=1)), out_dtype=BF16, epi=_epi_tanh,
               name="rwkv_decay_lora_in")
    la_in = mm(xa, bf(jnp.concatenate([iclr_a1[0], iclr_a1[1]], axis=1)), out_dtype=BF16, name="rwkv_iclr_lora_in")
    lg_in = mm(xg, bf(gate_g1), out_dtype=BF16, epi=_epi_sigmoid, name="rwkv_gate_lora_in")
    gate = mm(lg_in, bf(gate_g2), out_dtype=F32, name="rwkv_gate_lora_out")
    lw2 = mm(lw_in, bf(_block_diag2(dec_w2)), out_dtype=F32, epi=_epi_log_decay,
             extras=[(dec_w0.reshape(1, 2 * d), "col")], name="rwkv_log_decay")
    a2 = mm(la_in, bf(_block_diag2(iclr_a2)), out_dtype=F32, epi=_epi_bias_sigmoid,
            extras=[(iclr_a0.reshape(1, 2 * d), "col")], name="rwkv_iclr")
    k_a2 = k_a.reshape(1, d)
    yf, yb = wkv(r, k, v, lw2, a2, k_k.reshape(1, d), k_a2, seqs)
    yo = rwkv_post(yf, yb, r, k, v, a2, gate, k_a2, r_k.reshape(1, d), gn_w.reshape(1, d), gn_b.reshape(1, d))
    return mm(yo, bf(w_o), out_dtype=F32, epi=_epi_residual(2), extras=[(x, "row"), (mod3, "seq")], seqs=seqs,
              name="mixer_out_proj")


def attention_block(x, mod3, g, seqs, w_qkv, w_o, sink):
    d = x.shape[1]
    h = modnorm(x, g, mod3, seqs, row=0)
    qkv = mm(h, w_qkv.astype(BF16), out_dtype=BF16, name="attn_qkv")
    o = window_attention(qkv, sink, seqs, d=d)
    return mm(o, w_o.astype(BF16), out_dtype=F32, epi=_epi_residual(2), extras=[(x, "row"), (mod3, "seq")],
              seqs=seqs, name="mixer_out_proj")


def kernel(x_prompt, x_sample, c_prompt, c_sample, ada_w, ada_b, norm_mix, norm_ffn, rw_mu, rw_wr, rw_wk, rw_wv, rw_wo, rw_dec_w0, rw_dec_w1, rw_dec_w2, rw_iclr_a0, rw_iclr_a1, rw_iclr_a2, rw_gate_g1, rw_gate_g2, rw_k_k, rw_k_a, rw_r_k, rw_gn_w, rw_gn_b, at_w_qkv, at_w_o, at_sink, router_w, router_bias, moe_w_gate, moe_w_up, moe_w_down, sh_w_gate, sh_w_up, sh_w_down, final_norm):
    b1, s1, d = x_prompt.shape
    b2, s2, _ = x_sample.shape
    seqs = Seqs(b1, s1, b2, s2)
    depth = ada_w.shape[0]
    x = jnp.concatenate([x_prompt.reshape(b1 * s1, d), x_sample.reshape(b2 * s2, d)], axis=0)
    c = jnp.concatenate([c_prompt, c_sample], axis=0)
    rows = -(-seqs.nseq // TILE_SUBLANES) * TILE_SUBLANES
    c_act = jnp.pad(jax.nn.silu(c), ((0, rows - seqs.nseq), (0, 0)))
    for i in range(depth):
        j = i // 2
        mod = mm(c_act, ada_w[i], out_dtype=F32, epi=_epi_bias, extras=[(ada_b[i].reshape(1, -1), "col")],
                 nb=d, precision=lax.Precision.HIGHEST, name="adaln_mod")
        mod3 = mod[:seqs.nseq].reshape(seqs.nseq, 6, d)
        g_mix = norm_mix[i].reshape(1, d)
        if i % 2 == 0:
            x = rwkv_block(x, mod3, g_mix, seqs, rw_mu[j], rw_wr[j], rw_wk[j], rw_wv[j], rw_wo[j], rw_dec_w0[j],
                           rw_dec_w1[j], rw_dec_w2[j], rw_iclr_a0[j], rw_iclr_a1[j], rw_iclr_a2[j],
                           rw_gate_g1[j], rw_gate_g2[j], rw_k_k[j], rw_k_a[j], rw_r_k[j], rw_gn_w[j], rw_gn_b[j])
        else:
            x = attention_block(x, mod3, g_mix, seqs, at_w_qkv[j], at_w_o[j], at_sink[j])
        x = moe_block(x, mod3, norm_ffn[i].reshape(1, d), router_w[i], router_bias[i].reshape(1, -1),
                      moe_w_gate, moe_w_up, moe_w_down, sh_w_gate, sh_w_up, sh_w_down, i, seqs)
    fw = final_norm.reshape(1, d)
    y1 = final_rmsnorm(x, fw, 0, seqs.t1).reshape(b1, s1, d)
    y2 = final_rmsnorm(x, fw, seqs.t1, b2 * s2).reshape(b2, s2, d)
    return (y1, y2)
```
